```python
import math
import jax, jax.numpy as jnp
from jax import lax
import numpy as np

D_MODEL = 1024
BATCH = 8
SEQ = 4096
DEPTH = 1

N_META = 16
N_HEADS = 16
HEAD_DIM = 64
D_ATTN = N_HEADS * HEAD_DIM
D_CONV = D_MODEL
CONV_K = 31
Q_BLOCK = 128
LN_EPS = 1e-5
DEEPNORM_ALPHA = (2.0 * DEPTH) ** 0.25
DEEPNORM_BETA = (8.0 * DEPTH) ** -0.25
SPLITS = (D_ATTN, D_ATTN, D_ATTN,
          D_ATTN,
          D_CONV, D_CONV,
          D_CONV,
          D_MODEL, D_MODEL)
D_IN = sum(SPLITS)

kernel_name = "stickbreak_conformer_gated_hybrid"


def layer_norm(x, g, b):
    xf = x.astype(jnp.float32)
    mu = jnp.mean(xf, axis=-1, keepdims=True)
    var = jnp.mean(jnp.square(xf - mu), axis=-1, keepdims=True)
    return ((xf - mu) * lax.rsqrt(var + LN_EPS) * g.astype(jnp.float32)
            + b.astype(jnp.float32)).astype(x.dtype)


def stick_breaking_block(q_blk, k_pre, v_pre, q_start):
    nq, nk = q_blk.shape[2], k_pre.shape[2]
    z = jnp.einsum('bhqd,bhkd->bhqk', q_blk, k_pre).astype(jnp.float32) * (HEAD_DIM ** -0.5)
    q_pos = q_start + jnp.arange(nq)
    k_pos = jnp.arange(nk)
    mask = k_pos[None, :] < q_pos[:, None]
    log_beta = jax.nn.log_sigmoid(z)
    log_1m_beta = jnp.where(mask, log_beta - z, 0.0)
    csum = jnp.cumsum(log_1m_beta, axis=-1)
    suffix = csum[..., -1:] - csum
    attn = jnp.where(mask, jnp.exp(log_beta + suffix), 0.0)
    out = jnp.einsum('bhqk,bhkd->bhqd', attn, v_pre.astype(jnp.float32))
    return out.astype(q_blk.dtype)


def stick_breaking_attention(q, k, v):
    length = q.shape[2]
    bounds = [0, *range(N_META, length, Q_BLOCK), length]
    outs = []
    for lo, hi in zip(bounds[:-1], bounds[1:]):
        outs.append(stick_breaking_block(q[:, :, lo:hi], k[:, :, :hi], v[:, :, :hi], lo))
    return jnp.concatenate(outs, axis=2)


def causal_depthwise_conv(u, w, b):
    y = lax.conv_general_dilated(
        u, w[:, None, :].astype(u.dtype), window_strides=(1,), padding=[(CONV_K - 1, 0)],
        dimension_numbers=('NWC', 'WIO', 'NWC'), feature_group_count=u.shape[-1])
    return y + b


def hybrid_layer(x, w_in, dw_w, dw_b, conv_ln_g, conv_ln_b, w_attn_out, w_conv_out,
                 w_out, post_ln_g, post_ln_b):
    bsz, length, _ = x.shape
    proj = x @ w_in
    idx = np.cumsum(SPLITS)[:-1].tolist()
    q, k, v, z_a, u_c, g_c, z_c, gate_a, gate_c = jnp.split(proj, idx, axis=-1)

    def heads(t):
        return t.reshape(bsz, length, N_HEADS, HEAD_DIM).transpose(0, 2, 1, 3)
    o = stick_breaking_attention(heads(q), heads(k), heads(v))
    o = o.transpose(0, 2, 1, 3).reshape(bsz, length, D_ATTN)
    y_a = (o * jax.nn.silu(z_a)) @ w_attn_out

    u = u_c * jax.nn.sigmoid(g_c)
    c = causal_depthwise_conv(u, dw_w, dw_b)
    c = jax.nn.silu(layer_norm(c, conv_ln_g, conv_ln_b))
    y_c = (c * jax.nn.silu(z_c)) @ w_conv_out

    h = jax.nn.sigmoid(gate_a) * y_a + jax.nn.sigmoid(gate_c) * y_c
    out = h @ w_out
    return layer_norm(DEEPNORM_ALPHA * x + out, post_ln_g, post_ln_b)


def setup_inputs(seed: int = 0) -> dict:
    key = jax.random.key(seed)
    ks = jax.random.split(key, 16)
    f32 = jnp.float32
    x = jax.random.normal(ks[0], (BATCH, SEQ, D_MODEL), f32)
    meta_tokens = jax.random.normal(ks[1], (N_META, D_MODEL), f32)
    emb_ln_g = 1.0 + 0.02 * jax.random.normal(ks[2], (D_MODEL,), f32)
    emb_ln_b = 0.02 * jax.random.normal(ks[3], (D_MODEL,), f32)
    w_in = jax.random.normal(ks[4], (DEPTH, D_MODEL, D_IN), f32) * D_MODEL ** -0.5
    v_lo = 2 * D_ATTN
    col_scale = jnp.ones((D_IN,), f32).at[v_lo:v_lo + D_ATTN].set(DEEPNORM_BETA)
    w_in = w_in * col_scale
    dw_w = jax.random.normal(ks[5], (DEPTH, CONV_K, D_CONV), f32) * CONV_K ** -0.5
    dw_b = 0.02 * jax.random.normal(ks[6], (DEPTH, D_CONV), f32)
    conv_ln_g = 1.0 + 0.02 * jax.random.normal(ks[7], (DEPTH, D_CONV), f32)
    conv_ln_b = 0.02 * jax.random.normal(ks[8], (DEPTH, D_CONV), f32)
    w_attn_out = jax.random.normal(ks[9], (DEPTH, D_ATTN, D_MODEL), f32) * (D_ATTN ** -0.5 * DEEPNORM_BETA)
    w_conv_out = jax.random.normal(ks[10], (DEPTH, D_CONV, D_MODEL), f32) * (D_CONV ** -0.5 * DEEPNORM_BETA)
    w_out = jax.random.normal(ks[11], (DEPTH, D_MODEL, D_MODEL), f32) * (D_MODEL ** -0.5 * DEEPNORM_BETA)
    post_ln_g = 1.0 + 0.02 * jax.random.normal(ks[12], (DEPTH, D_MODEL), f32)
    post_ln_b = 0.02 * jax.random.normal(ks[13], (DEPTH, D_MODEL), f32)
    return {"x": x, "meta_tokens": meta_tokens, "emb_ln_g": emb_ln_g, "emb_ln_b": emb_ln_b,
            "w_in": w_in, "dw_w": dw_w, "dw_b": dw_b, "conv_ln_g": conv_ln_g,
            "conv_ln_b": conv_ln_b, "w_attn_out": w_attn_out, "w_conv_out": w_conv_out,
            "w_out": w_out, "post_ln_g": post_ln_g, "post_ln_b": post_ln_b}


def reference(x, meta_tokens, emb_ln_g, emb_ln_b, w_in, dw_w, dw_b, conv_ln_g, conv_ln_b,
              w_attn_out, w_conv_out, w_out, post_ln_g, post_ln_b):
    bsz = x.shape[0]
    meta = jnp.broadcast_to(meta_tokens.astype(x.dtype)[None], (bsz, N_META, x.shape[-1]))
    h = jnp.concatenate([meta, x], axis=1)
    h = layer_norm(h, emb_ln_g, emb_ln_b)
    for layer in range(DEPTH):
        h = hybrid_layer(h, w_in[layer], dw_w[layer], dw_b[layer], conv_ln_g[layer],
                         conv_ln_b[layer], w_attn_out[layer], w_conv_out[layer],
                         w_out[layer], post_ln_g[layer], post_ln_b[layer])
    return h[:, N_META:]
```

```python
import functools

import jax
import jax.numpy as jnp
from jax import lax
from jax.experimental import pallas as pl
from jax.experimental.pallas import tpu as pltpu

F32 = jnp.float32
BF16 = jnp.bfloat16

N_META = 16
N_HEADS = 16
HEAD_DIM = 64
CONV_K = 31
LN_EPS = 1e-5
N_SPLITS = 9

LANES = 128
BLK = 128
PAIR_W = 2 * HEAD_DIM
HALO = 32
EXP_ZERO_BELOW = -104.0
VMEM_LIMIT = 56 * 1024 * 1024


def _layer_norm(x, g, b):
    mu = jnp.mean(x, axis=-1, keepdims=True)
    xc = x - mu
    var = jnp.mean(xc * xc, axis=-1, keepdims=True)
    return xc * lax.rsqrt(var + LN_EPS) * g + b


def _sigmoid(x):
    return 1.0 / (1.0 + jnp.exp(-x))


def _ln_proj_kernel(x_ref, g_ref, b_ref, w_ref, q_ref, k_ref, v_ref, sa_ref, u_ref,
                    sc_ref, ga_ref, gc_ref, *, d, nc):
    hn = _layer_norm(x_ref[...], g_ref[...], b_ref[...]).astype(BF16)

    def proj(split, c):
        lo = split * d + c * nc
        return jnp.dot(hn, w_ref[:, lo:lo + nc], preferred_element_type=F32)

    for c in range(d // nc):
        cols = slice(c * nc, (c + 1) * nc)
        q_ref[:, cols] = (proj(0, c) * (HEAD_DIM ** -0.5)).astype(BF16)
        k_ref[:, cols] = proj(1, c).astype(BF16)
        v_ref[:, cols] = proj(2, c).astype(BF16)
        z_a = proj(3, c)
        sa_ref[:, cols] = z_a * _sigmoid(z_a)
        u_ref[:, cols] = proj(4, c) * _sigmoid(proj(5, c))
        z_c = proj(6, c)
        sc_ref[:, cols] = z_c * _sigmoid(z_c)
        ga_ref[:, cols] = _sigmoid(proj(7, c))
        gc_ref[:, cols] = _sigmoid(proj(8, c))


def _ln_proj(x2d, g, b, w_bf16, tm):
    m, d = x2d.shape
    row = lambda i: (i, 0)
    const = lambda i: (0, 0)
    out_sds = lambda dt: jax.ShapeDtypeStruct((m, d), dt)
    return pl.pallas_call(
        functools.partial(_ln_proj_kernel, d=d, nc=min(d, 512)),
        grid=(m // tm,),
        in_specs=[
            pl.BlockSpec((tm, d), row),
            pl.BlockSpec((1, d), const),
            pl.BlockSpec((1, d), const),
            pl.BlockSpec((d, N_SPLITS * d), const, pipeline_mode=pl.Buffered(1)),
        ],
        out_specs=[pl.BlockSpec((tm, d), row)] * 8,
        out_shape=[out_sds(BF16)] * 3 + [out_sds(F32)] * 5,
        compiler_params=pltpu.CompilerParams(
            dimension_semantics=("arbitrary",), vmem_limit_bytes=VMEM_LIMIT),
        name="ln_proj",
    )(x2d, g, b, w_bf16)


def _sb_tile(qz, k2, v2, r, mm, mask):
    z = lax.dot_general(qz, k2, (((1,), (1,)), ((), ())), preferred_element_type=F32)
    sp = jnp.log(1.0 + jnp.exp(-jnp.abs(z)))
    log_beta = jnp.minimum(z, 0.0) - sp
    l1m = log_beta - z
    if mask is not None:
        l1m = jnp.where(mask, l1m, 0.0)
    hi = l1m.astype(BF16)
    lo = (l1m - hi.astype(F32)).astype(BF16)
    st = jnp.dot(jnp.concatenate([hi, lo], axis=1), mm, preferred_element_type=F32)
    suffix, total = st[:, :BLK], st[:, BLK:]
    attn = jnp.exp(log_beta + suffix + r)
    if mask is not None:
        attn = jnp.where(mask, attn, 0.0)
    pv = jnp.dot(attn.astype(BF16), v2, preferred_element_type=F32)
    return pv, r + total


def _attention_kernel(q_ref, k_ref, v_ref, km_ref, vm_ref, sa_ref, mm_ref, o_ref, *, n_pairs):
    seq = q_ref.shape[1]
    mm = mm_ref[...]
    lane = lax.broadcasted_iota(jnp.int32, (BLK, LANES), 1)
    row = lax.broadcasted_iota(jnp.int32, (BLK, BLK), 0)
    col = lax.broadcasted_iota(jnp.int32, (BLK, BLK), 1)
    first_head = lane < HEAD_DIM
    tri = col < row
    meta_valid = col >= BLK - N_META
    n_chain = 2 * n_pairs

    def kv(ref, start, p):
        return ref[0, pl.ds(start, BLK), p * PAIR_W:(p + 1) * PAIR_W]

    def q_block(qb, carry):
        row0 = pl.multiple_of(qb * BLK, BLK)
        qz = []
        for p in range(n_pairs):
            q2 = kv(q_ref, row0, p)
            qz.append(jnp.where(first_head, q2, jnp.zeros_like(q2)))
            qz.append(jnp.where(first_head, jnp.zeros_like(q2), q2))

        def sweep(start, accs, rs, mask):
            new_accs, new_rs = [], []
            for c in range(n_chain):
                p = c // 2
                pv, r = _sb_tile(qz[c], kv(k_ref, start, p), kv(v_ref, start, p), rs[c], mm, mask)
                new_accs.append(accs[c] + pv)
                new_rs.append(r)
            return new_accs, new_rs

        def all_dead(rs):
            worst = rs[0]
            for r in rs[1:]:
                worst = jnp.maximum(worst, r)
            return jnp.max(worst) < EXP_ZERO_BELOW

        zeros = [jnp.zeros((BLK, LANES), F32)] * n_chain
        accs, rs = sweep(row0, zeros, zeros, tri)

        def cond(state):
            i, done, _, _ = state
            return jnp.logical_and(i < qb // 2, jnp.logical_not(done))

        def body(state):
            i, _, accs, rs = state
            start = pl.multiple_of(row0 - (2 * i + 1) * BLK, BLK)
            accs, rs = sweep(start, accs, rs, None)
            accs, rs = sweep(pl.multiple_of(start - BLK, BLK), accs, rs, None)
            return i + 1, all_dead(rs), accs, rs

        _, done, accs, rs = lax.while_loop(cond, body, (0, all_dead(rs), accs, rs))

        def tail(accs, rs):
            first_valid = jnp.logical_and(col >= 0, (qb % 2) == 1)
            accs, rs = sweep(0, accs, rs, first_valid)
            new_accs = []
            for c in range(n_chain):
                p = c // 2
                cols = slice(p * PAIR_W, (p + 1) * PAIR_W)
                pv, _ = _sb_tile(qz[c], km_ref[:, cols], vm_ref[:, cols], rs[c], mm, meta_valid)
                new_accs.append(accs[c] + pv)
            return new_accs

        accs = lax.cond(done, lambda a, r: a, tail, accs, rs)

        for p in range(n_pairs):
            o = jnp.where(first_head, accs[2 * p], accs[2 * p + 1])
            cols = slice(p * PAIR_W, (p + 1) * PAIR_W)
            o_ref[0, pl.ds(row0, BLK), cols] = (o * sa_ref[0, pl.ds(row0, BLK), cols]).astype(BF16)
        return carry

    lax.fori_loop(0, seq // BLK, q_block, 0)


def _attention(q, k, v, km, vm, sa, mm, n_pairs):
    bsz, seq, d = q.shape
    w = n_pairs * PAIR_W
    seq_blk = pl.BlockSpec((1, seq, w), lambda b, g: (b, 0, g))
    meta_blk = pl.BlockSpec((BLK, w), lambda b, g: (0, g))
    return pl.pallas_call(
        functools.partial(_attention_kernel, n_pairs=n_pairs),
        grid=(bsz, d // w),
        in_specs=[seq_blk, seq_blk, seq_blk, meta_blk, meta_blk, seq_blk,
                  pl.BlockSpec((2 * BLK, 2 * BLK), lambda b, g: (0, 0))],
        out_specs=seq_blk,
        out_shape=jax.ShapeDtypeStruct((bsz, seq, d), BF16),
        compiler_params=pltpu.CompilerParams(
            dimension_semantics=("arbitrary", "arbitrary"), vmem_limit_bytes=VMEM_LIMIT),
        name="sb_attention",
    )(q, k, v, km, vm, sa, mm)


def _tail_kernel(og_ref, u_ref, uprev_ref, umeta_ref, sc_ref, ga_ref, gc_ref, x_ref,
                 dww_ref, dwb_ref, cg_ref, cb_ref, eg_ref, eb_ref, wa_ref, wc_ref, wo_ref,
                 pg_ref, pb_ref, out_ref, ubuf, cbuf, *, tm, alpha, rows):
    i = pl.program_id(1)
    d = u_ref.shape[2]

    @pl.when(i == 0)
    def _():
        ubuf[0:HALO, :] = umeta_ref[...]

    @pl.when(i > 0)
    def _():
        ubuf[0:HALO, :] = uprev_ref[0]

    ubuf[HALO:HALO + tm, :] = u_ref[0]

    base = HALO - (CONV_K - 1)
    for r0 in range(0, tm, rows):
        for c0 in range(0, d, LANES):
            acc = jnp.zeros((rows, LANES), F32)
            for k in range(CONV_K):
                acc = acc + dww_ref[k:k + 1, c0:c0 + LANES] * ubuf[r0 + base + k:r0 + base + k + rows,
                                                                    c0:c0 + LANES]
            cbuf[r0:r0 + rows, c0:c0 + LANES] = acc + dwb_ref[:, c0:c0 + LANES]

    cn = _layer_norm(cbuf[...], cg_ref[...], cb_ref[...])
    cg = (cn * _sigmoid(cn) * sc_ref[0]).astype(BF16)
    y_c = jnp.dot(cg, wc_ref[...], preferred_element_type=F32)
    y_a = jnp.dot(og_ref[0], wa_ref[...], preferred_element_type=F32)
    h = (ga_ref[0] * y_a + gc_ref[0] * y_c).astype(BF16)
    out = jnp.dot(h, wo_ref[...], preferred_element_type=F32)
    res = _layer_norm(x_ref[0], eg_ref[...], eb_ref[...])
    out_ref[0] = _layer_norm(alpha * res + out, pg_ref[...], pb_ref[...])


def _tail(og, u, umeta, sc, ga, gc, x, dww, dwb, cg, cb, eg, eb, wa, wc, wo, pg, pb, tm, alpha):
    bsz, seq, d = x.shape
    tile = pl.BlockSpec((1, tm, d), lambda b, i: (b, i, 0))
    prev = pl.BlockSpec((1, HALO, d), lambda b, i: (b, jnp.maximum(i * (tm // HALO) - 1, 0), 0))
    const = lambda shape: pl.BlockSpec(shape, lambda b, i: (0, 0))
    vec = const((1, d))
    return pl.pallas_call(
        functools.partial(_tail_kernel, tm=tm, alpha=alpha, rows=8),
        grid=(bsz, seq // tm),
        in_specs=[tile, tile, prev, const((HALO, d)), tile, tile, tile, tile,
                  const((CONV_K, d)), vec, vec, vec, vec, vec,
                  const((d, d)), const((d, d)), const((d, d)), vec, vec],
        out_specs=tile,
        out_shape=jax.ShapeDtypeStruct((bsz, seq, d), F32),
        scratch_shapes=[pltpu.VMEM((HALO + tm, d), F32), pltpu.VMEM((tm, d), F32)],
        compiler_params=pltpu.CompilerParams(
            dimension_semantics=("arbitrary", "arbitrary"), vmem_limit_bytes=VMEM_LIMIT),
        name="conv_merge_out",
    )(og, u, u, umeta, sc, ga, gc, x, dww, dwb, cg, cb, eg, eb, wa, wc, wo, pg, pb)


def _suffix_sum_matrix():
    j = lax.broadcasted_iota(jnp.int32, (BLK, BLK), 0)
    s = lax.broadcasted_iota(jnp.int32, (BLK, BLK), 1)
    half = jnp.concatenate([(j > s).astype(BF16), jnp.ones((BLK, BLK), BF16)], axis=1)
    return jnp.concatenate([half, half], axis=0)


def kernel(x, meta_tokens, emb_ln_g, emb_ln_b, w_in, dw_w, dw_b, conv_ln_g, conv_ln_b,
           w_attn_out, w_conv_out, w_out, post_ln_g, post_ln_b):
    bsz, seq, d = x.shape
    depth = w_in.shape[0]
    assert depth == 1 and d == N_HEADS * HEAD_DIM and w_in.shape[2] == N_SPLITS * d
    assert meta_tokens.shape == (N_META, d) and dw_w.shape[1] == CONV_K
    assert seq % 512 == 0
    alpha = (2.0 * depth) ** 0.25
    vec = lambda a: a.reshape(1, d).astype(F32)
    eg, eb = vec(emb_ln_g), vec(emb_ln_b)
    w_in_bf = w_in[0].astype(BF16)

    q, k, v, sa, u, sc, ga, gc = _ln_proj(x.reshape(bsz * seq, d), eg, eb, w_in_bf, tm=512)
    _, km, vm, _, um, _, _, _ = _ln_proj(meta_tokens, eg, eb, w_in_bf, tm=N_META)
    shape3 = lambda a: a.reshape(bsz, seq, d)

    pad_meta = lambda a, n: jnp.pad(a, ((n - N_META, 0), (0, 0)))
    og = _attention(shape3(q), shape3(k), shape3(v), pad_meta(km, BLK), pad_meta(vm, BLK),
                    shape3(sa), _suffix_sum_matrix(), n_pairs=2)

    return _tail(og, shape3(u), pad_meta(um, HALO), shape3(sc), shape3(ga), shape3(gc), x,
                 dw_w[0], vec(dw_b[0]), vec(conv_ln_g[0]), vec(conv_ln_b[0]), eg, eb,
                 w_attn_out[0].astype(BF16), w_conv_out[0].astype(BF16), w_out[0].astype(BF16),
                 vec(post_ln_g[0]), vec(post_ln_b[0]), tm=256, alpha=alpha)
```

```python
import functools

import jax
import jax.numpy as jnp
from jax import lax
from jax.experimental import pallas as pl
from jax.experimental.pallas import tpu as pltpu

F32 = jnp.float32
BF16 = jnp.bfloat16

N_META = 16
N_HEADS = 16
HEAD_DIM = 64
CONV_K = 31
LN_EPS = 1e-5
N_SPLITS = 9

LANES = 128
SUBLANES = 8
BLK = 128
PAIR_W = 2 * HEAD_DIM
HALO = 32
LOG2E = 1.4426950408889634
Q_SCALE = -LOG2E * HEAD_DIM ** -0.5
EXP2_ZERO_BELOW = -151.0
VMEM_LIMIT = 56 * 1024 * 1024


def _layer_norm(x, g, b):
    mu = jnp.mean(x, axis=-1, keepdims=True)
    xc = x - mu
    var = jnp.mean(xc * xc, axis=-1, keepdims=True)
    return xc * lax.rsqrt(var + LN_EPS) * g + b


def _sigmoid(x):
    return 1.0 / (1.0 + jnp.exp(-x))


def _ln_proj_kernel(x_ref, g_ref, b_ref, w_ref, q_ref, k_ref, v_ref, sa_ref, u_ref,
                    sc_ref, ga_ref, gc_ref, *, d, nc):
    hn = _layer_norm(x_ref[...], g_ref[...], b_ref[...]).astype(BF16)

    def proj(split, c):
        lo = split * d + c * nc
        return jnp.dot(hn, w_ref[:, lo:lo + nc], preferred_element_type=F32)

    for c in range(d // nc):
        cols = slice(c * nc, (c + 1) * nc)
        q_ref[:, cols] = (proj(0, c) * Q_SCALE).astype(BF16)
        k_ref[:, cols] = proj(1, c).astype(BF16)
        v_ref[:, cols] = proj(2, c).astype(BF16)
        z_a = proj(3, c)
        sa_ref[:, cols] = z_a * _sigmoid(z_a)
        u_ref[:, cols] = proj(4, c) * _sigmoid(proj(5, c))
        z_c = proj(6, c)
        sc_ref[:, cols] = z_c * _sigmoid(z_c)
        ga_ref[:, cols] = _sigmoid(proj(7, c))
        gc_ref[:, cols] = _sigmoid(proj(8, c))


def _ln_proj(x2d, g, b, w_bf16, tm):
    m, d = x2d.shape
    row = lambda i: (i, 0)
    const = lambda i: (0, 0)
    out_sds = lambda dt: jax.ShapeDtypeStruct((m, d), dt)
    return pl.pallas_call(
        functools.partial(_ln_proj_kernel, d=d, nc=min(d, 512)),
        grid=(m // tm,),
        in_specs=[
            pl.BlockSpec((tm, d), row),
            pl.BlockSpec((1, d), const),
            pl.BlockSpec((1, d), const),
            pl.BlockSpec((d, N_SPLITS * d), const, pipeline_mode=pl.Buffered(1)),
        ],
        out_specs=[pl.BlockSpec((tm, d), row)] * 8,
        out_shape=[out_sds(BF16)] * 3 + [out_sds(F32)] * 5,
        compiler_params=pltpu.CompilerParams(
            dimension_semantics=("arbitrary",), vmem_limit_bytes=VMEM_LIMIT),
        name="ln_proj",
    )(x2d, g, b, w_bf16)


def _sb_span(qq, k_span, v_span, r, mm, masks):
    nb = len(masks)
    nz = lax.dot_general(qq, k_span, (((1,), (1,)), ((), ())), preferred_element_type=F32)
    nzs, parts = [], []
    for j in range(nb):
        nzj = nz[:, j * BLK:(j + 1) * BLK]
        neg_abs = lax.bitcast_convert_type(
            lax.bitcast_convert_type(nzj, jnp.uint32) | jnp.uint32(0x80000000), F32)
        sp = jnp.log(1.0 + jnp.exp2(neg_abs))
        l1m = jnp.minimum(nzj, 0.0) - sp * LOG2E
        if masks[j] is not None:
            l1m = jnp.where(masks[j], l1m, 0.0)
        hi = l1m.astype(BF16)
        lo = (l1m - hi.astype(F32)).astype(BF16)
        nzs.append(nzj)
        parts.append(jnp.concatenate([hi, lo], axis=1))
    st = jnp.dot(jnp.concatenate(parts, axis=0), mm, preferred_element_type=F32)
    attn = [None] * nb
    for j in reversed(range(nb)):
        st_j = st[j * 2 * BLK:(j + 1) * 2 * BLK]
        a = jnp.exp2(st_j[:, :BLK] - nzs[j] + r)
        if masks[j] is not None:
            a = jnp.where(masks[j], a, 0.0)
        attn[j] = a.astype(BF16)
        r = r + st_j[:, BLK:]
    pv = jnp.dot(jnp.concatenate(attn, axis=1), v_span, preferred_element_type=F32)
    return pv, r


def _attention_kernel(q_ref, k_ref, v_ref, km_ref, vm_ref, sa_ref, mm_ref, o_ref, *, n_pairs):
    seq = q_ref.shape[1]
    mm = mm_ref[...]
    lane = lax.broadcasted_iota(jnp.int32, (BLK, LANES), 1)
    row = lax.broadcasted_iota(jnp.int32, (2 * BLK, BLK), 0) % BLK
    col = lax.broadcasted_iota(jnp.int32, (2 * BLK, BLK), 1)
    first_head = lane < HEAD_DIM
    tri = col < row
    meta_valid = col >= BLK - N_META
    pair_cols = [slice(p * PAIR_W, (p + 1) * PAIR_W) for p in range(n_pairs)]
    zeros = [jnp.zeros((2 * BLK, LANES), F32)] * n_pairs

    def all_dead(rs):
        worst = rs[0]
        for r in rs[1:]:
            worst = jnp.maximum(worst, r)
        return jnp.max(worst) < EXP2_ZERO_BELOW

    def two_q_blocks(m, carry):
        qbs = [2 * m, 2 * m + 1]
        row0s = [pl.multiple_of(qb * BLK, BLK) for qb in qbs]
        qqs = []
        for row0 in row0s:
            qq = []
            for p in range(n_pairs):
                q2 = q_ref[0, pl.ds(row0, BLK), pair_cols[p]]
                zero = jnp.zeros_like(q2)
                qq.append(jnp.concatenate([jnp.where(first_head, q2, zero),
                                           jnp.where(first_head, zero, q2)], axis=0))
            qqs.append(qq)

        def sweep(qq, start, masks, accs, rs):
            n = len(masks) * BLK
            out = [_sb_span(qq[p], k_ref[0, pl.ds(start, n), pair_cols[p]],
                            v_ref[0, pl.ds(start, n), pair_cols[p]], rs[p], mm, masks)
                   for p in range(n_pairs)]
            return [a + o[0] for a, o in zip(accs, out)], [o[1] for o in out]

        def heads_general():
            return [sweep(qqs[s], pl.multiple_of(row0s[s] - 2 * BLK, BLK), [None, None, tri], zeros, zeros)
                    for s in range(2)]

        def heads_first():
            return [sweep(qqs[0], 0, [tri], zeros, zeros), sweep(qqs[1], 0, [None, tri], zeros, zeros)]

        states = lax.cond(m >= 1, heads_general, heads_first)

        for s in range(2):
            qb, row0, qq = qbs[s], row0s[s], qqs[s]
            accs, rs = states[s]
            left = jnp.where(m >= 1, qb - 2, 0)

            def cond(state):
                i, done, _, _ = state
                return jnp.logical_and(i < left // 2, jnp.logical_not(done))

            def body(state):
                i, _, accs, rs = state
                start = pl.multiple_of(row0 - (2 * i + 4) * BLK, BLK)
                accs, rs = sweep(qq, start, [None, None], accs, rs)
                return i + 1, all_dead(rs), accs, rs

            _, done, accs, rs = lax.while_loop(cond, body, (0, all_dead(rs), accs, rs))

            def tail(accs, rs):
                first_valid = jnp.logical_and(col >= 0, (left % 2) == 1)
                new_accs = []
                for p in range(n_pairs):
                    k_span = jnp.concatenate([km_ref[:, pair_cols[p]], k_ref[0, 0:BLK, pair_cols[p]]], axis=0)
                    v_span = jnp.concatenate([vm_ref[:, pair_cols[p]], v_ref[0, 0:BLK, pair_cols[p]]], axis=0)
                    pv, _ = _sb_span(qq[p], k_span, v_span, rs[p], mm, [meta_valid, first_valid])
                    new_accs.append(accs[p] + pv)
                return new_accs

            accs = lax.cond(done, lambda a, r: a, tail, accs, rs)

            for p in range(n_pairs):
                o = jnp.where(first_head, accs[p][:BLK], accs[p][BLK:])
                gate = sa_ref[0, pl.ds(row0, BLK), pair_cols[p]]
                o_ref[0, pl.ds(row0, BLK), pair_cols[p]] = (o * gate).astype(BF16)
        return carry

    lax.fori_loop(0, seq // (2 * BLK), two_q_blocks, 0)


def _attention(q, k, v, km, vm, sa, mm, n_pairs):
    bsz, seq, d = q.shape
    w = n_pairs * PAIR_W
    seq_blk = pl.BlockSpec((1, seq, w), lambda b, g: (b, 0, g))
    meta_blk = pl.BlockSpec((BLK, w), lambda b, g: (0, g))
    return pl.pallas_call(
        functools.partial(_attention_kernel, n_pairs=n_pairs),
        grid=(bsz, d // w),
        in_specs=[seq_blk, seq_blk, seq_blk, meta_blk, meta_blk, seq_blk,
                  pl.BlockSpec((2 * BLK, 2 * BLK), lambda b, g: (0, 0))],
        out_specs=seq_blk,
        out_shape=jax.ShapeDtypeStruct((bsz, seq, d), BF16),
        compiler_params=pltpu.CompilerParams(
            dimension_semantics=("arbitrary", "arbitrary"), vmem_limit_bytes=VMEM_LIMIT),
        name="sb_attention",
    )(q, k, v, km, vm, sa, mm)


def _tail_kernel(og_ref, u_ref, uprev_ref, umeta_ref, sc_ref, ga_ref, gc_ref, x_ref,
                 dww_ref, dwb_ref, cg_ref, cb_ref, eg_ref, eb_ref, wa_ref, wc_ref, wo_ref,
                 pg_ref, pb_ref, out_ref, ubuf, cbuf, *, tm, alpha, rows):
    i = pl.program_id(1)
    d = u_ref.shape[2]

    @pl.when(i == 0)
    def _():
        ubuf[0:HALO, :] = umeta_ref[...]

    @pl.when(i > 0)
    def _():
        ubuf[0:HALO, :] = uprev_ref[0]

    ubuf[HALO:HALO + tm, :] = u_ref[0]

    base = HALO - (CONV_K - 1)

    def lane_tile(ci, carry):
        cols = pl.ds(pl.multiple_of(ci * LANES, LANES), LANES)
        bias = dwb_ref[:, cols]
        span = rows + HALO
        for r0 in range(0, tm, rows):
            block = ubuf[r0:r0 + span, cols]
            acc = jnp.zeros((rows, LANES), F32)
            for phase in range(SUBLANES):
                shifted = block if phase == 0 else pltpu.roll(block, span - phase, axis=0)
                for s in range(base, base + CONV_K):
                    if s % SUBLANES == phase:
                        off = s - phase
                        acc = acc + dww_ref[s - base:s - base + 1, cols] * shifted[off:off + rows]
            cbuf[r0:r0 + rows, cols] = acc + bias
        return carry

    lax.fori_loop(0, d // LANES, lane_tile, 0)

    cn = _layer_norm(cbuf[...], cg_ref[...], cb_ref[...])
    cg = (cn * _sigmoid(cn) * sc_ref[0]).astype(BF16)
    y_c = jnp.dot(cg, wc_ref[...], preferred_element_type=F32)
    y_a = jnp.dot(og_ref[0], wa_ref[...], preferred_element_type=F32)
    h = (ga_ref[0] * y_a + gc_ref[0] * y_c).astype(BF16)
    out = jnp.dot(h, wo_ref[...], preferred_element_type=F32)
    res = _layer_norm(x_ref[0], eg_ref[...], eb_ref[...])
    out_ref[0] = _layer_norm(alpha * res + out, pg_ref[...], pb_ref[...])


def _tail(og, u, umeta, sc, ga, gc, x, dww, dwb, cg, cb, eg, eb, wa, wc, wo, pg, pb, tm, alpha):
    bsz, seq, d = x.shape
    tile = pl.BlockSpec((1, tm, d), lambda b, i: (b, i, 0))
    prev = pl.BlockSpec((1, HALO, d), lambda b, i: (b, jnp.maximum(i * (tm // HALO) - 1, 0), 0))
    const = lambda shape: pl.BlockSpec(shape, lambda b, i: (0, 0))
    vec = const((1, d))
    return pl.pallas_call(
        functools.partial(_tail_kernel, tm=tm, alpha=alpha, rows=64),
        grid=(bsz, seq // tm),
        in_specs=[tile, tile, prev, const((HALO, d)), tile, tile, tile, tile,
                  const((CONV_K, d)), vec, vec, vec, vec, vec,
                  const((d, d)), const((d, d)), const((d, d)), vec, vec],
        out_specs=tile,
        out_shape=jax.ShapeDtypeStruct((bsz, seq, d), F32),
        scratch_shapes=[pltpu.VMEM((HALO + tm, d), F32), pltpu.VMEM((tm, d), F32)],
        compiler_params=pltpu.CompilerParams(
            dimension_semantics=("arbitrary", "arbitrary"), vmem_limit_bytes=VMEM_LIMIT),
        name="conv_merge_out",
    )(og, u, u, umeta, sc, ga, gc, x, dww, dwb, cg, cb, eg, eb, wa, wc, wo, pg, pb)


def _suffix_sum_matrix():
    j = lax.broadcasted_iota(jnp.int32, (BLK, BLK), 0)
    s = lax.broadcasted_iota(jnp.int32, (BLK, BLK), 1)
    half = jnp.concatenate([(j >= s).astype(BF16), jnp.ones((BLK, BLK), BF16)], axis=1)
    return jnp.concatenate([half, half], axis=0)


def kernel(x, meta_tokens, emb_ln_g, emb_ln_b, w_in, dw_w, dw_b, conv_ln_g, conv_ln_b,
           w_attn_out, w_conv_out, w_out, post_ln_g, post_ln_b):
    bsz, seq, d = x.shape
    depth = w_in.shape[0]
    assert depth == 1 and d == N_HEADS * HEAD_DIM and w_in.shape[2] == N_SPLITS * d
    assert meta_tokens.shape == (N_META, d) and dw_w.shape[1] == CONV_K
    assert seq % 512 == 0
    alpha = (2.0 * depth) ** 0.25
    vec = lambda a: a.reshape(1, d).astype(F32)
    eg, eb = vec(emb_ln_g), vec(emb_ln_b)
    w_in_bf = w_in[0].astype(BF16)

    q, k, v, sa, u, sc, ga, gc = _ln_proj(x.reshape(bsz * seq, d), eg, eb, w_in_bf, tm=512)
    _, km, vm, _, um, _, _, _ = _ln_proj(meta_tokens, eg, eb, w_in_bf, tm=N_META)
    shape3 = lambda a: a.reshape(bsz, seq, d)

    pad_meta = lambda a, n: jnp.pad(a, ((n - N_META, 0), (0, 0)))
    og = _attention(shape3(q), shape3(k), shape3(v), pad_meta(km, BLK), pad_meta(vm, BLK),
                    shape3(sa), _suffix_sum_matrix(), n_pairs=2)

    return _tail(og, shape3(u), pad_meta(um, HALO), shape3(sc), shape3(ga), shape3(gc), x,
                 dw_w[0], vec(dw_b[0]), vec(conv_ln_g[0]), vec(conv_ln_b[0]), eg, eb,
                 w_attn_out[0].astype(BF16), w_conv_out[0].astype(BF16), w_out[0].astype(BF16),
                 vec(post_ln_g[0]), vec(post_ln_b[0]), tm=256, alpha=alpha)
```

```python
import functools

import jax
import jax.numpy as jnp
from jax import lax
from jax.experimental import pallas as pl
from jax.experimental.pallas import tpu as pltpu

F32 = jnp.float32
BF16 = jnp.bfloat16

N_META = 16
N_HEADS = 16
HEAD_DIM = 64
CONV_K = 31
LN_EPS = 1e-5
N_SPLITS = 9

LANES = 128
SUBLANES = 8
BLK = 128
PAIR_W = 2 * HEAD_DIM
HALO = 32
LOG2E = 1.4426950408889634
Q_SCALE = -LOG2E * HEAD_DIM ** -0.5
EXP2_ZERO_BELOW = -127.0
Q_GROUP = 4
VMEM_LIMIT = 56 * 1024 * 1024


def _layer_norm(x, g, b):
    mu = jnp.mean(x, axis=-1, keepdims=True)
    xc = x - mu
    var = jnp.mean(xc * xc, axis=-1, keepdims=True)
    return xc * lax.rsqrt(var + LN_EPS) * g + b


def _sigmoid(x):
    return 1.0 / (1.0 + jnp.exp(-x))


def _ln_proj_kernel(x_ref, g_ref, b_ref, w_ref, q_ref, k_ref, v_ref, sa_ref, u_ref,
                    sc_ref, ga_ref, gc_ref, *, d, nc):
    hn = _layer_norm(x_ref[...], g_ref[...], b_ref[...]).astype(BF16)

    def proj(split, c):
        lo = split * d + c * nc
        return jnp.dot(hn, w_ref[:, lo:lo + nc], preferred_element_type=F32)

    for c in range(d // nc):
        cols = slice(c * nc, (c + 1) * nc)
        q_ref[:, cols] = (proj(0, c) * Q_SCALE).astype(BF16)
        k_ref[:, cols] = proj(1, c).astype(BF16)
        v_ref[:, cols] = proj(2, c).astype(BF16)
        z_a = proj(3, c)
        sa_ref[:, cols] = z_a * _sigmoid(z_a)
        u_ref[:, cols] = proj(4, c) * _sigmoid(proj(5, c))
        z_c = proj(6, c)
        sc_ref[:, cols] = z_c * _sigmoid(z_c)
        ga_ref[:, cols] = _sigmoid(proj(7, c))
        gc_ref[:, cols] = _sigmoid(proj(8, c))


def _ln_proj(x2d, g, b, w_bf16, tm):
    m, d = x2d.shape
    row = lambda i: (i, 0)
    const = lambda i: (0, 0)
    out_sds = lambda dt: jax.ShapeDtypeStruct((m, d), dt)
    return pl.pallas_call(
        functools.partial(_ln_proj_kernel, d=d, nc=min(d, 512)),
        grid=(m // tm,),
        in_specs=[
            pl.BlockSpec((tm, d), row),
            pl.BlockSpec((1, d), const),
            pl.BlockSpec((1, d), const),
            pl.BlockSpec((d, N_SPLITS * d), const, pipeline_mode=pl.Buffered(1)),
        ],
        out_specs=[pl.BlockSpec((tm, d), row)] * 8,
        out_shape=[out_sds(BF16)] * 3 + [out_sds(F32)] * 5,
        compiler_params=pltpu.CompilerParams(
            dimension_semantics=("arbitrary",), vmem_limit_bytes=VMEM_LIMIT),
        name="ln_proj",
    )(x2d, g, b, w_bf16)


def _sb_span(qq, k_span, v_span, r, mm, masks):
    nb = len(masks)
    nz = lax.dot_general(qq, k_span, (((1,), (1,)), ((), ())), preferred_element_type=F32)
    nzs, parts = [], []
    for j in range(nb):
        nzj = nz[:, j * BLK:(j + 1) * BLK]
        neg_abs = lax.bitcast_convert_type(
            lax.bitcast_convert_type(nzj, jnp.uint32) | jnp.uint32(0x80000000), F32)
        sp = jnp.log(1.0 + jnp.exp2(neg_abs))
        l1m = jnp.minimum(nzj, 0.0) - sp * LOG2E
        if masks[j] is not None:
            l1m = jnp.where(masks[j], l1m, 0.0)
        hi = l1m.astype(BF16)
        lo = (l1m - hi.astype(F32)).astype(BF16)
        nzs.append(nzj)
        parts.append(jnp.concatenate([hi, lo], axis=1))
    st = jnp.dot(jnp.concatenate(parts, axis=0), mm, preferred_element_type=F32)
    attn = [None] * nb
    for j in reversed(range(nb)):
        st_j = st[j * 2 * BLK:(j + 1) * 2 * BLK]
        a = jnp.exp2(st_j[:, :BLK] - nzs[j] + r)
        if masks[j] is not None:
            a = jnp.where(masks[j], a, 0.0)
        attn[j] = a.astype(BF16)
        r = r + st_j[:, BLK:]
    pv = jnp.dot(jnp.concatenate(attn, axis=1), v_span, preferred_element_type=F32)
    return pv, r


def _attention_kernel(q_ref, k_ref, v_ref, km_ref, vm_ref, sa_ref, mm_ref, o_ref, done_ref, *, n_pairs):
    seq = q_ref.shape[1]
    mm = mm_ref[...]
    lane = lax.broadcasted_iota(jnp.int32, (BLK, LANES), 1)
    row = lax.broadcasted_iota(jnp.int32, (2 * BLK, BLK), 0) % BLK
    col = lax.broadcasted_iota(jnp.int32, (2 * BLK, BLK), 1)
    first_head = lane < HEAD_DIM
    tri = col < row
    meta_valid = col >= BLK - N_META
    pair_cols = [slice(p * PAIR_W, (p + 1) * PAIR_W) for p in range(n_pairs)]
    zeros = [jnp.zeros((2 * BLK, LANES), F32)] * n_pairs

    def load_queries(row0):
        qq = []
        for p in range(n_pairs):
            q2 = q_ref[0, pl.ds(row0, BLK), pair_cols[p]]
            zero = jnp.zeros_like(q2)
            qq.append(jnp.concatenate([jnp.where(first_head, q2, zero),
                                       jnp.where(first_head, zero, q2)], axis=0))
        return qq

    def sweep(qq, start, masks, accs, rs):
        n = len(masks) * BLK
        out = [_sb_span(qq[p], k_ref[0, pl.ds(start, n), pair_cols[p]],
                        v_ref[0, pl.ds(start, n), pair_cols[p]], rs[p], mm, masks)
               for p in range(n_pairs)]
        return [a + o[0] for a, o in zip(accs, out)], [o[1] for o in out]

    def all_dead(rs):
        worst = rs[0]
        for r in rs[1:]:
            worst = jnp.maximum(worst, r)
        return jnp.max(worst) < EXP2_ZERO_BELOW

    def emit(row0, accs):
        for p in range(n_pairs):
            o = jnp.where(first_head, accs[p][:BLK], accs[p][BLK:])
            gate = sa_ref[0, pl.ds(row0, BLK), pair_cols[p]]
            o_ref[0, pl.ds(row0, BLK), pair_cols[p]] = (o * gate).astype(BF16)

    def full_sweep(qb):
        row0 = pl.multiple_of(qb * BLK, BLK)
        qq = load_queries(row0)
        accs, rs = sweep(qq, row0, [tri], zeros, zeros)

        def cond(state):
            i, done, _, _ = state
            return jnp.logical_and(i < qb // 2, jnp.logical_not(done))

        def body(state):
            i, _, accs, rs = state
            start = pl.multiple_of(row0 - (2 * i + 2) * BLK, BLK)
            accs, rs = sweep(qq, start, [None, None], accs, rs)
            return i + 1, all_dead(rs), accs, rs

        _, done, accs, rs = lax.while_loop(cond, body, (0, all_dead(rs), accs, rs))

        def tail(accs, rs):
            first_valid = jnp.logical_and(col >= 0, (qb % 2) == 1)
            new_accs = []
            for p in range(n_pairs):
                k_span = jnp.concatenate([km_ref[:, pair_cols[p]], k_ref[0, 0:BLK, pair_cols[p]]], axis=0)
                v_span = jnp.concatenate([vm_ref[:, pair_cols[p]], v_ref[0, 0:BLK, pair_cols[p]]], axis=0)
                pv, _ = _sb_span(qq[p], k_span, v_span, rs[p], mm, [meta_valid, first_valid])
                new_accs.append(accs[p] + pv)
            return new_accs

        emit(row0, lax.cond(done, lambda a, r: a, tail, accs, rs))

    def q_group(m, carry):
        for s in range(Q_GROUP):
            row0 = pl.multiple_of((Q_GROUP * m + s) * BLK, BLK)
            has_left = m >= 1 if s == 0 else True
            start = pl.multiple_of(jnp.maximum(row0 - BLK, 0), BLK) if s == 0 else pl.multiple_of(row0 - BLK, BLK)
            accs, rs = sweep(load_queries(row0), start, [None, tri], zeros, zeros)
            emit(row0, accs)
            done_ref[s] = jnp.logical_and(all_dead(rs), has_left).astype(jnp.int32)

        def redo(s, c):
            @pl.when(done_ref[s] == 0)
            def _():
                full_sweep(Q_GROUP * m + s)
            return c

        lax.fori_loop(0, Q_GROUP, redo, 0)
        return carry

    lax.fori_loop(0, seq // (Q_GROUP * BLK), q_group, 0)


def _attention(q, k, v, km, vm, sa, mm, n_pairs):
    bsz, seq, d = q.shape
    w = n_pairs * PAIR_W
    seq_blk = pl.BlockSpec((1, seq, w), lambda b, g: (b, 0, g))
    meta_blk = pl.BlockSpec((BLK, w), lambda b, g: (0, g))
    return pl.pallas_call(
        functools.partial(_attention_kernel, n_pairs=n_pairs),
        grid=(bsz, d // w),
        in_specs=[seq_blk, seq_blk, seq_blk, meta_blk, meta_blk, seq_blk,
                  pl.BlockSpec((2 * BLK, 2 * BLK), lambda b, g: (0, 0))],
        out_specs=seq_blk,
        out_shape=jax.ShapeDtypeStruct((bsz, seq, d), BF16),
        scratch_shapes=[pltpu.SMEM((Q_GROUP,), jnp.int32)],
        compiler_params=pltpu.CompilerParams(
            dimension_semantics=("arbitrary", "arbitrary"), vmem_limit_bytes=VMEM_LIMIT),
        name="sb_attention",
    )(q, k, v, km, vm, sa, mm)


def _tail_kernel(og_ref, u_ref, uprev_ref, umeta_ref, sc_ref, ga_ref, gc_ref, x_ref,
                 dww_ref, dwb_ref, cg_ref, cb_ref, eg_ref, eb_ref, wa_ref, wc_ref, wo_ref,
                 pg_ref, pb_ref, out_ref, ubuf, cbuf, *, tm, alpha, rows):
    i = pl.program_id(1)
    d = u_ref.shape[2]

    @pl.when(i == 0)
    def _():
        ubuf[0:HALO, :] = umeta_ref[...]

    @pl.when(i > 0)
    def _():
        ubuf[0:HALO, :] = uprev_ref[0]

    ubuf[HALO:HALO + tm, :] = u_ref[0]

    base = HALO - (CONV_K - 1)

    def lane_tile(ci, carry):
        cols = pl.ds(pl.multiple_of(ci * LANES, LANES), LANES)
        bias = dwb_ref[:, cols]
        span = rows + HALO
        for r0 in range(0, tm, rows):
            block = ubuf[r0:r0 + span, cols]
            acc = jnp.zeros((rows, LANES), F32)
            for phase in range(SUBLANES):
                shifted = block if phase == 0 else pltpu.roll(block, span - phase, axis=0)
                for s in range(base, base + CONV_K):
                    if s % SUBLANES == phase:
                        off = s - phase
                        acc = acc + dww_ref[s - base:s - base + 1, cols] * shifted[off:off + rows]
            cbuf[r0:r0 + rows, cols] = acc + bias
        return carry

    lax.fori_loop(0, d // LANES, lane_tile, 0)

    cn = _layer_norm(cbuf[...], cg_ref[...], cb_ref[...])
    cg = (cn * _sigmoid(cn) * sc_ref[0]).astype(BF16)
    y_c = jnp.dot(cg, wc_ref[...], preferred_element_type=F32)
    y_a = jnp.dot(og_ref[0], wa_ref[...], preferred_element_type=F32)
    h = (ga_ref[0] * y_a + gc_ref[0] * y_c).astype(BF16)
    out = jnp.dot(h, wo_ref[...], preferred_element_type=F32)
    res = _layer_norm(x_ref[0], eg_ref[...], eb_ref[...])
    out_ref[0] = _layer_norm(alpha * res + out, pg_ref[...], pb_ref[...])


def _tail(og, u, umeta, sc, ga, gc, x, dww, dwb, cg, cb, eg, eb, wa, wc, wo, pg, pb, tm, alpha):
    bsz, seq, d = x.shape
    tile = pl.BlockSpec((1, tm, d), lambda b, i: (b, i, 0))
    prev = pl.BlockSpec((1, HALO, d), lambda b, i: (b, jnp.maximum(i * (tm // HALO) - 1, 0), 0))
    const = lambda shape: pl.BlockSpec(shape, lambda b, i: (0, 0))
    vec = const((1, d))
    return pl.pallas_call(
        functools.partial(_tail_kernel, tm=tm, alpha=alpha, rows=64),
        grid=(bsz, seq // tm),
        in_specs=[tile, tile, prev, const((HALO, d)), tile, tile, tile, tile,
                  const((CONV_K, d)), vec, vec, vec, vec, vec,
                  const((d, d)), const((d, d)), const((d, d)), vec, vec],
        out_specs=tile,
        out_shape=jax.ShapeDtypeStruct((bsz, seq, d), F32),
        scratch_shapes=[pltpu.VMEM((HALO + tm, d), F32), pltpu.VMEM((tm, d), F32)],
        compiler_params=pltpu.CompilerParams(
            dimension_semantics=("arbitrary", "arbitrary"), vmem_limit_bytes=VMEM_LIMIT),
        name="conv_merge_out",
    )(og, u, u, umeta, sc, ga, gc, x, dww, dwb, cg, cb, eg, eb, wa, wc, wo, pg, pb)


def _suffix_sum_matrix():
    j = lax.broadcasted_iota(jnp.int32, (BLK, BLK), 0)
    s = lax.broadcasted_iota(jnp.int32, (BLK, BLK), 1)
    half = jnp.concatenate([(j >= s).astype(BF16), jnp.ones((BLK, BLK), BF16)], axis=1)
    return jnp.concatenate([half, half], axis=0)


def kernel(x, meta_tokens, emb_ln_g, emb_ln_b, w_in, dw_w, dw_b, conv_ln_g, conv_ln_b,
           w_attn_out, w_conv_out, w_out, post_ln_g, post_ln_b):
    bsz, seq, d = x.shape
    depth = w_in.shape[0]
    assert depth == 1 and d == N_HEADS * HEAD_DIM and w_in.shape[2] == N_SPLITS * d
    assert meta_tokens.shape == (N_META, d) and dw_w.shape[1] == CONV_K
    assert seq % 512 == 0
    alpha = (2.0 * depth) ** 0.25
    vec = lambda a: a.reshape(1, d).astype(F32)
    eg, eb = vec(emb_ln_g), vec(emb_ln_b)
    w_in_bf = w_in[0].astype(BF16)

    q, k, v, sa, u, sc, ga, gc = _ln_proj(x.reshape(bsz * seq, d), eg, eb, w_in_bf, tm=512)
    _, km, vm, _, um, _, _, _ = _ln_proj(meta_tokens, eg, eb, w_in_bf, tm=N_META)
    shape3 = lambda a: a.reshape(bsz, seq, d)

    pad_meta = lambda a, n: jnp.pad(a, ((n - N_META, 0), (0, 0)))
    og = _attention(shape3(q), shape3(k), shape3(v), pad_meta(km, BLK), pad_meta(vm, BLK),
                    shape3(sa), _suffix_sum_matrix(), n_pairs=2)

    return _tail(og, shape3(u), pad_meta(um, HALO), shape3(sc), shape3(ga), shape3(gc), x,
                 dw_w[0], vec(dw_b[0]), vec(conv_ln_g[0]), vec(conv_ln_b[0]), eg, eb,
                 w_attn_out[0].astype(BF16), w_conv_out[0].astype(BF16), w_out[0].astype(BF16),
                 vec(post_ln_g[0]), vec(post_ln_b[0]), tm=256, alpha=alpha)
```

```python
import functools

import jax
import jax.numpy as jnp
from jax import lax
from jax.experimental import pallas as pl
from jax.experimental.pallas import tpu as pltpu

F32 = jnp.float32
BF16 = jnp.bfloat16

N_META = 16
N_HEADS = 16
HEAD_DIM = 64
CONV_K = 31
LN_EPS = 1e-5
N_SPLITS = 9

LANES = 128
SUBLANES = 8
BLK = 128
PAIR_W = 2 * HEAD_DIM
HALO = 32
LOG2E = 1.4426950408889634
Q_SCALE = -LOG2E * HEAD_DIM ** -0.5
EXP2_ZERO_BELOW = -127.0
Q_GROUP = 4
VMEM_LIMIT = 56 * 1024 * 1024


def _layer_norm(x, g, b):
    mu = jnp.mean(x, axis=-1, keepdims=True)
    xc = x - mu
    var = jnp.mean(xc * xc, axis=-1, keepdims=True)
    return xc * lax.rsqrt(var + LN_EPS) * g + b


def _sigmoid(x):
    return 1.0 / (1.0 + jnp.exp(-x))


def _ln_proj_kernel(x_ref, g_ref, b_ref, w_ref, q_ref, k_ref, v_ref, sa_ref, u_ref,
                    sc_ref, ga_ref, gc_ref, *, d, nc):
    hn = _layer_norm(x_ref[...], g_ref[...], b_ref[...]).astype(BF16)

    def proj(split, c):
        lo = split * d + c * nc
        return jnp.dot(hn, w_ref[:, lo:lo + nc], preferred_element_type=F32)

    for c in range(d // nc):
        cols = slice(c * nc, (c + 1) * nc)
        q_ref[:, cols] = (proj(0, c) * Q_SCALE).astype(BF16)
        k_ref[:, cols] = proj(1, c).astype(BF16)
        v_ref[:, cols] = proj(2, c).astype(BF16)
        z_a = proj(3, c)
        sa_ref[:, cols] = z_a * _sigmoid(z_a)
        u_ref[:, cols] = proj(4, c) * _sigmoid(proj(5, c))
        z_c = proj(6, c)
        sc_ref[:, cols] = z_c * _sigmoid(z_c)
        ga_ref[:, cols] = _sigmoid(proj(7, c))
        gc_ref[:, cols] = _sigmoid(proj(8, c))


def _ln_proj(x2d, g, b, w_bf16, tm):
    m, d = x2d.shape
    row = lambda i: (i, 0)
    const = lambda i: (0, 0)
    out_sds = lambda dt: jax.ShapeDtypeStruct((m, d), dt)
    return pl.pallas_call(
        functools.partial(_ln_proj_kernel, d=d, nc=min(d, 512)),
        grid=(m // tm,),
        in_specs=[
            pl.BlockSpec((tm, d), row),
            pl.BlockSpec((1, d), const),
            pl.BlockSpec((1, d), const),
            pl.BlockSpec((d, N_SPLITS * d), const, pipeline_mode=pl.Buffered(1)),
        ],
        out_specs=[pl.BlockSpec((tm, d), row)] * 8,
        out_shape=[out_sds(BF16)] * 3 + [out_sds(F32)] * 5,
        compiler_params=pltpu.CompilerParams(
            dimension_semantics=("arbitrary",), vmem_limit_bytes=VMEM_LIMIT),
        name="ln_proj",
    )(x2d, g, b, w_bf16)


def _sb_span(qq, k_span, v_span, r, mm, masks):
    nb = len(masks)
    nz = lax.dot_general(qq, k_span, (((1,), (1,)), ((), ())), preferred_element_type=F32)
    nzs, parts = [], []
    for j in range(nb):
        nzj = nz[:, j * BLK:(j + 1) * BLK]
        neg_abs = lax.bitcast_convert_type(
            lax.bitcast_convert_type(nzj, jnp.uint32) | jnp.uint32(0x80000000), F32)
        sp = jnp.log(1.0 + jnp.exp2(neg_abs))
        l1m = jnp.minimum(nzj, 0.0) - sp * LOG2E
        if masks[j] is not None:
            l1m = jnp.where(masks[j], l1m, 0.0)
        hi = l1m.astype(BF16)
        lo = (l1m - hi.astype(F32)).astype(BF16)
        nzs.append(nzj)
        parts.append(jnp.concatenate([hi, lo], axis=1))
    st = jnp.dot(jnp.concatenate(parts, axis=0), mm, preferred_element_type=F32)
    attn = [None] * nb
    for j in reversed(range(nb)):
        st_j = st[j * 2 * BLK:(j + 1) * 2 * BLK]
        a = jnp.exp2(st_j[:, :BLK] - nzs[j] + r)
        if masks[j] is not None:
            a = jnp.where(masks[j], a, 0.0)
        attn[j] = a.astype(BF16)
        r = r + st_j[:, BLK:]
    pv = jnp.dot(jnp.concatenate(attn, axis=1), v_span, preferred_element_type=F32)
    return pv, r


def _attention_kernel(q_ref, k_ref, v_ref, km_ref, vm_ref, sa_ref, mm_ref, o_ref, done_ref, acc_ref, r_ref,
                      *, n_pairs):
    seq = q_ref.shape[1]
    mm = mm_ref[...]
    lane = lax.broadcasted_iota(jnp.int32, (BLK, LANES), 1)
    row = lax.broadcasted_iota(jnp.int32, (2 * BLK, BLK), 0) % BLK
    col = lax.broadcasted_iota(jnp.int32, (2 * BLK, BLK), 1)
    first_head = lane < HEAD_DIM
    tri = col < row
    meta_valid = col >= BLK - N_META
    pair_cols = [slice(p * PAIR_W, (p + 1) * PAIR_W) for p in range(n_pairs)]
    zeros = [jnp.zeros((2 * BLK, LANES), F32)] * n_pairs

    def load_queries(row0):
        qq = []
        for p in range(n_pairs):
            q2 = q_ref[0, pl.ds(row0, BLK), pair_cols[p]]
            zero = jnp.zeros_like(q2)
            qq.append(jnp.concatenate([jnp.where(first_head, q2, zero),
                                       jnp.where(first_head, zero, q2)], axis=0))
        return qq

    def key_rows(starts):
        out = []
        for p in range(n_pairs):
            ks = [k_ref[0, pl.ds(st, BLK), pair_cols[p]] for st in starts]
            vs = [v_ref[0, pl.ds(st, BLK), pair_cols[p]] for st in starts]
            out.append((jnp.concatenate(ks, axis=0), jnp.concatenate(vs, axis=0)))
        return out

    def sweep(qq, kvs, masks, accs, rs):
        out = [_sb_span(qq[p], kvs[p][0], kvs[p][1], rs[p], mm, masks) for p in range(n_pairs)]
        return [a + o[0] for a, o in zip(accs, out)], [o[1] for o in out]

    def all_dead(rs):
        worst = rs[0]
        for r in rs[1:]:
            worst = jnp.maximum(worst, r)
        return jnp.max(worst) < EXP2_ZERO_BELOW

    def emit(row0, accs):
        for p in range(n_pairs):
            o = jnp.where(first_head, accs[p][:BLK], accs[p][BLK:])
            gate = sa_ref[0, pl.ds(row0, BLK), pair_cols[p]]
            o_ref[0, pl.ds(row0, BLK), pair_cols[p]] = (o * gate).astype(BF16)

    def extend(qb, s):
        row0 = pl.multiple_of(qb * BLK, BLK)
        qq = load_queries(row0)
        accs = [acc_ref[s, p] for p in range(n_pairs)]
        rs = [r_ref[s, p] for p in range(n_pairs)]
        left = jnp.maximum(qb - 1, 0)

        def cond(state):
            i, done, _, _ = state
            return jnp.logical_and(i < left // 2, jnp.logical_not(done))

        def body(state):
            i, _, accs, rs = state
            start = pl.multiple_of(row0 - (2 * i + 3) * BLK, BLK)
            accs, rs = sweep(qq, key_rows([start, start + BLK]), [None, None], accs, rs)
            return i + 1, all_dead(rs), accs, rs

        _, done, accs, rs = lax.while_loop(cond, body, (0, False, accs, rs))

        def tail(accs, rs):
            first_valid = jnp.logical_and(col >= 0, (left % 2) == 1)
            new_accs = []
            for p in range(n_pairs):
                k_span = jnp.concatenate([km_ref[:, pair_cols[p]], k_ref[0, 0:BLK, pair_cols[p]]], axis=0)
                v_span = jnp.concatenate([vm_ref[:, pair_cols[p]], v_ref[0, 0:BLK, pair_cols[p]]], axis=0)
                pv, _ = _sb_span(qq[p], k_span, v_span, rs[p], mm, [meta_valid, first_valid])
                new_accs.append(accs[p] + pv)
            return new_accs

        emit(row0, lax.cond(done, lambda a, r: a, tail, accs, rs))

    def q_group(m, carry):
        for s in range(Q_GROUP):
            row0 = pl.multiple_of((Q_GROUP * m + s) * BLK, BLK)
            if s == 0:
                left_start = pl.multiple_of(jnp.maximum(row0 - BLK, 0), BLK)
                masks = [jnp.logical_and(col >= 0, m >= 1), tri]
            else:
                left_start, masks = pl.multiple_of(row0 - BLK, BLK), [None, tri]
            accs, rs = sweep(load_queries(row0), key_rows([left_start, row0]), masks, zeros, zeros)
            emit(row0, accs)
            for p in range(n_pairs):
                acc_ref[s, p] = accs[p]
                r_ref[s, p] = rs[p]
            done_ref[s] = all_dead(rs).astype(jnp.int32)

        def finish(s, c):
            @pl.when(done_ref[s] == 0)
            def _():
                extend(Q_GROUP * m + s, s)
            return c

        lax.fori_loop(0, Q_GROUP, finish, 0)
        return carry

    lax.fori_loop(0, seq // (Q_GROUP * BLK), q_group, 0)


def _attention(q, k, v, km, vm, sa, mm, n_pairs):
    bsz, seq, d = q.shape
    w = n_pairs * PAIR_W
    seq_blk = pl.BlockSpec((1, seq, w), lambda b, g: (b, 0, g))
    meta_blk = pl.BlockSpec((BLK, w), lambda b, g: (0, g))
    return pl.pallas_call(
        functools.partial(_attention_kernel, n_pairs=n_pairs),
        grid=(bsz, d // w),
        in_specs=[seq_blk, seq_blk, seq_blk, meta_blk, meta_blk, seq_blk,
                  pl.BlockSpec((2 * BLK, 2 * BLK), lambda b, g: (0, 0))],
        out_specs=seq_blk,
        out_shape=jax.ShapeDtypeStruct((bsz, seq, d), BF16),
        scratch_shapes=[pltpu.SMEM((Q_GROUP,), jnp.int32),
                        pltpu.VMEM((Q_GROUP, n_pairs, 2 * BLK, LANES), F32),
                        pltpu.VMEM((Q_GROUP, n_pairs, 2 * BLK, LANES), F32)],
        compiler_params=pltpu.CompilerParams(
            dimension_semantics=("arbitrary", "arbitrary"), vmem_limit_bytes=VMEM_LIMIT),
        name="sb_attention",
    )(q, k, v, km, vm, sa, mm)


def _tail_kernel(og_ref, u_ref, uprev_ref, umeta_ref, sc_ref, ga_ref, gc_ref, x_ref,
                 dww_ref, dwb_ref, cg_ref, cb_ref, eg_ref, eb_ref, wa_ref, wc_ref, wo_ref,
                 pg_ref, pb_ref, out_ref, ubuf, cbuf, *, tm, alpha, rows):
    i = pl.program_id(1)
    d = u_ref.shape[2]

    @pl.when(i == 0)
    def _():
        ubuf[0:HALO, :] = umeta_ref[...]

    @pl.when(i > 0)
    def _():
        ubuf[0:HALO, :] = uprev_ref[0]

    ubuf[HALO:HALO + tm, :] = u_ref[0]

    base = HALO - (CONV_K - 1)

    def lane_tile(ci, carry):
        cols = pl.ds(pl.multiple_of(ci * LANES, LANES), LANES)
        bias = dwb_ref[:, cols]
        span = rows + HALO
        for r0 in range(0, tm, rows):
            block = ubuf[r0:r0 + span, cols]
            acc = jnp.zeros((rows, LANES), F32)
            for phase in range(SUBLANES):
                shifted = block if phase == 0 else pltpu.roll(block, span - phase, axis=0)
                for s in range(base, base + CONV_K):
                    if s % SUBLANES == phase:
                        off = s - phase
                        acc = acc + dww_ref[s - base:s - base + 1, cols] * shifted[off:off + rows]
            cbuf[r0:r0 + rows, cols] = acc + bias
        return carry

    lax.fori_loop(0, d // LANES, lane_tile, 0)

    cn = _layer_norm(cbuf[...], cg_ref[...], cb_ref[...])
    cg = (cn * _sigmoid(cn) * sc_ref[0]).astype(BF16)
    y_c = jnp.dot(cg, wc_ref[...], preferred_element_type=F32)
    y_a = jnp.dot(og_ref[0], wa_ref[...], preferred_element_type=F32)
    h = (ga_ref[0] * y_a + gc_ref[0] * y_c).astype(BF16)
    out = jnp.dot(h, wo_ref[...], preferred_element_type=F32)
    res = _layer_norm(x_ref[0], eg_ref[...], eb_ref[...])
    out_ref[0] = _layer_norm(alpha * res + out, pg_ref[...], pb_ref[...])


def _tail(og, u, umeta, sc, ga, gc, x, dww, dwb, cg, cb, eg, eb, wa, wc, wo, pg, pb, tm, alpha):
    bsz, seq, d = x.shape
    tile = pl.BlockSpec((1, tm, d), lambda b, i: (b, i, 0))
    prev = pl.BlockSpec((1, HALO, d), lambda b, i: (b, jnp.maximum(i * (tm // HALO) - 1, 0), 0))
    const = lambda shape: pl.BlockSpec(shape, lambda b, i: (0, 0))
    vec = const((1, d))
    return pl.pallas_call(
        functools.partial(_tail_kernel, tm=tm, alpha=alpha, rows=64),
        grid=(bsz, seq // tm),
        in_specs=[tile, tile, prev, const((HALO, d)), tile, tile, tile, tile,
                  const((CONV_K, d)), vec, vec, vec, vec, vec,
                  const((d, d)), const((d, d)), const((d, d)), vec, vec],
        out_specs=tile,
        out_shape=jax.ShapeDtypeStruct((bsz, seq, d), F32),
        scratch_shapes=[pltpu.VMEM((HALO + tm, d), F32), pltpu.VMEM((tm, d), F32)],
        compiler_params=pltpu.CompilerParams(
            dimension_semantics=("arbitrary", "arbitrary"), vmem_limit_bytes=VMEM_LIMIT),
        name="conv_merge_out",
    )(og, u, u, umeta, sc, ga, gc, x, dww, dwb, cg, cb, eg, eb, wa, wc, wo, pg, pb)


def _suffix_sum_matrix():
    j = lax.broadcasted_iota(jnp.int32, (BLK, BLK), 0)
    s = lax.broadcasted_iota(jnp.int32, (BLK, BLK), 1)
    half = jnp.concatenate([(j >= s).astype(BF16), jnp.ones((BLK, BLK), BF16)], axis=1)
    return jnp.concatenate([half, half], axis=0)


def kernel(x, meta_tokens, emb_ln_g, emb_ln_b, w_in, dw_w, dw_b, conv_ln_g, conv_ln_b,
           w_attn_out, w_conv_out, w_out, post_ln_g, post_ln_b):
    bsz, seq, d = x.shape
    depth = w_in.shape[0]
    assert depth == 1 and d == N_HEADS * HEAD_DIM and w_in.shape[2] == N_SPLITS * d
    assert meta_tokens.shape == (N_META, d) and dw_w.shape[1] == CONV_K
    assert seq % 512 == 0
    alpha = (2.0 * depth) ** 0.25
    vec = lambda a: a.reshape(1, d).astype(F32)
    eg, eb = vec(emb_ln_g), vec(emb_ln_b)
    w_in_bf = w_in[0].astype(BF16)

    q, k, v, sa, u, sc, ga, gc = _ln_proj(x.reshape(bsz * seq, d), eg, eb, w_in_bf, tm=512)
    _, km, vm, _, um, _, _, _ = _ln_proj(meta_tokens, eg, eb, w_in_bf, tm=N_META)
    shape3 = lambda a: a.reshape(bsz, seq, d)

    pad_meta = lambda a, n: jnp.pad(a, ((n - N_META, 0), (0, 0)))
    og = _attention(shape3(q), shape3(k), shape3(v), pad_meta(km, BLK), pad_meta(vm, BLK),
                    shape3(sa), _suffix_sum_matrix(), n_pairs=2)

    return _tail(og, shape3(u), pad_meta(um, HALO), shape3(sc), shape3(ga), shape3(gc), x,
                 dw_w[0], vec(dw_b[0]), vec(conv_ln_g[0]), vec(conv_ln_b[0]), eg, eb,
                 w_attn_out[0].astype(BF16), w_conv_out[0].astype(BF16), w_out[0].astype(BF16),
                 vec(post_ln_g[0]), vec(post_ln_b[0]), tm=512, alpha=alpha)
```

```python
import functools

import jax
import jax.numpy as jnp
from jax import lax
from jax.experimental import pallas as pl
from jax.experimental.pallas import tpu as pltpu

F32 = jnp.float32
BF16 = jnp.bfloat16

N_META = 16
N_HEADS = 16
HEAD_DIM = 64
CONV_K = 31
LN_EPS = 1e-5
N_SPLITS = 9

LANES = 128
SUBLANES = 8
BLK = 128
PAIR_W = 2 * HEAD_DIM
HALO = 32
LOG2E = 1.4426950408889634
Q_SCALE = -LOG2E * HEAD_DIM ** -0.5
EXP2_ZERO_BELOW = -127.0
Q_GROUP = 8
VMEM_LIMIT = 56 * 1024 * 1024


def _layer_norm(x, g, b):
    mu = jnp.mean(x, axis=-1, keepdims=True)
    xc = x - mu
    var = jnp.mean(xc * xc, axis=-1, keepdims=True)
    return xc * lax.rsqrt(var + LN_EPS) * g + b


def _sigmoid(x):
    return 1.0 / (1.0 + jnp.exp(-x))


def _ln_proj_kernel(x_ref, g_ref, b_ref, w_ref, q_ref, k_ref, v_ref, sa_ref, u_ref,
                    sc_ref, ga_ref, gc_ref, res_ref, *, d, nc, alpha):
    hn = _layer_norm(x_ref[...], g_ref[...], b_ref[...])
    res_ref[...] = alpha * hn
    hn = hn.astype(BF16)

    def proj(split, c):
        lo = split * d + c * nc
        return jnp.dot(hn, w_ref[:, lo:lo + nc], preferred_element_type=F32)

    for c in range(d // nc):
        cols = slice(c * nc, (c + 1) * nc)
        q_ref[:, cols] = (proj(0, c) * Q_SCALE).astype(BF16)
        k_ref[:, cols] = proj(1, c).astype(BF16)
        v_ref[:, cols] = proj(2, c).astype(BF16)
        z_a = proj(3, c)
        sa_ref[:, cols] = z_a * _sigmoid(z_a)
        u_ref[:, cols] = proj(4, c) * _sigmoid(proj(5, c))
        z_c = proj(6, c)
        sc_ref[:, cols] = z_c * _sigmoid(z_c)
        ga_ref[:, cols] = _sigmoid(proj(7, c))
        gc_ref[:, cols] = _sigmoid(proj(8, c))


def _ln_proj(x2d, g, b, w_bf16, tm, alpha):
    m, d = x2d.shape
    row = lambda i: (i, 0)
    const = lambda i: (0, 0)
    out_sds = lambda dt: jax.ShapeDtypeStruct((m, d), dt)
    return pl.pallas_call(
        functools.partial(_ln_proj_kernel, d=d, nc=min(d, 512), alpha=alpha),
        grid=(m // tm,),
        in_specs=[
            pl.BlockSpec((tm, d), row),
            pl.BlockSpec((1, d), const),
            pl.BlockSpec((1, d), const),
            pl.BlockSpec((d, N_SPLITS * d), const, pipeline_mode=pl.Buffered(1)),
        ],
        out_specs=[pl.BlockSpec((tm, d), row)] * 9,
        out_shape=[out_sds(BF16)] * 3 + [out_sds(F32)] * 6,
        compiler_params=pltpu.CompilerParams(
            dimension_semantics=("arbitrary",), vmem_limit_bytes=VMEM_LIMIT),
        name="ln_proj",
    )(x2d, g, b, w_bf16)


def _sb_span(qq, k_span, v_span, r, mm, masks):
    nb = len(masks)
    nz = lax.dot_general(qq, k_span, (((1,), (1,)), ((), ())), preferred_element_type=F32)
    nzs, parts = [], []
    for j in range(nb):
        nzj = nz[:, j * BLK:(j + 1) * BLK]
        neg_abs = lax.bitcast_convert_type(
            lax.bitcast_convert_type(nzj, jnp.uint32) | jnp.uint32(0x80000000), F32)
        sp = jnp.log(1.0 + jnp.exp2(neg_abs))
        l1m = jnp.minimum(nzj, 0.0) - sp * LOG2E
        if masks[j] is not None:
            l1m = jnp.where(masks[j], l1m, 0.0)
        hi = l1m.astype(BF16)
        lo = (l1m - hi.astype(F32)).astype(BF16)
        nzs.append(nzj)
        parts.append(jnp.concatenate([hi, lo], axis=1))
    st = jnp.dot(jnp.concatenate(parts, axis=0), mm, preferred_element_type=F32)
    attn = [None] * nb
    for j in reversed(range(nb)):
        st_j = st[j * 2 * BLK:(j + 1) * 2 * BLK]
        a = jnp.exp2(st_j[:, :BLK] - nzs[j] + r)
        if masks[j] is not None:
            a = jnp.where(masks[j], a, 0.0)
        attn[j] = a.astype(BF16)
        r = r + st_j[:, BLK:]
    pv = jnp.dot(jnp.concatenate(attn, axis=1), v_span, preferred_element_type=F32)
    return pv, r


def _attention_kernel(q_ref, k_ref, v_ref, km_ref, vm_ref, sa_ref, mm_ref, o_ref, done_ref, acc_ref, r_ref,
                      *, n_pairs):
    seq = q_ref.shape[1]
    mm = mm_ref[...]
    lane = lax.broadcasted_iota(jnp.int32, (BLK, LANES), 1)
    row = lax.broadcasted_iota(jnp.int32, (2 * BLK, BLK), 0) % BLK
    col = lax.broadcasted_iota(jnp.int32, (2 * BLK, BLK), 1)
    first_head = lane < HEAD_DIM
    tri = col < row
    meta_valid = col >= BLK - N_META
    pair_cols = [slice(p * PAIR_W, (p + 1) * PAIR_W) for p in range(n_pairs)]
    zeros = [jnp.zeros((2 * BLK, LANES), F32)] * n_pairs

    def load_queries(row0):
        qq = []
        for p in range(n_pairs):
            q2 = q_ref[0, pl.ds(row0, BLK), pair_cols[p]]
            zero = jnp.zeros_like(q2)
            qq.append(jnp.concatenate([jnp.where(first_head, q2, zero),
                                       jnp.where(first_head, zero, q2)], axis=0))
        return qq

    def key_rows(starts):
        out = []
        for p in range(n_pairs):
            ks = [k_ref[0, pl.ds(st, BLK), pair_cols[p]] for st in starts]
            vs = [v_ref[0, pl.ds(st, BLK), pair_cols[p]] for st in starts]
            out.append((jnp.concatenate(ks, axis=0), jnp.concatenate(vs, axis=0)))
        return out

    def sweep(qq, kvs, masks, accs, rs):
        out = [_sb_span(qq[p], kvs[p][0], kvs[p][1], rs[p], mm, masks) for p in range(n_pairs)]
        return [a + o[0] for a, o in zip(accs, out)], [o[1] for o in out]

    def all_dead(rs):
        worst = rs[0]
        for r in rs[1:]:
            worst = jnp.maximum(worst, r)
        return jnp.max(worst) < EXP2_ZERO_BELOW

    def emit(row0, accs):
        for p in range(n_pairs):
            o = jnp.where(first_head, accs[p][:BLK], accs[p][BLK:])
            gate = sa_ref[0, pl.ds(row0, BLK), pair_cols[p]]
            o_ref[0, pl.ds(row0, BLK), pair_cols[p]] = (o * gate).astype(BF16)

    def extend(qb, s):
        row0 = pl.multiple_of(qb * BLK, BLK)
        qq = load_queries(row0)
        accs = [acc_ref[s, p] for p in range(n_pairs)]
        rs = [r_ref[s, p] for p in range(n_pairs)]
        left = jnp.maximum(qb - 1, 0)

        def cond(state):
            i, done, _, _ = state
            return jnp.logical_and(i < left // 2, jnp.logical_not(done))

        def body(state):
            i, _, accs, rs = state
            start = pl.multiple_of(row0 - (2 * i + 3) * BLK, BLK)
            accs, rs = sweep(qq, key_rows([start, start + BLK]), [None, None], accs, rs)
            return i + 1, all_dead(rs), accs, rs

        _, done, accs, rs = lax.while_loop(cond, body, (0, False, accs, rs))

        def tail(accs, rs):
            first_valid = jnp.logical_and(col >= 0, (left % 2) == 1)
            new_accs = []
            for p in range(n_pairs):
                k_span = jnp.concatenate([km_ref[:, pair_cols[p]], k_ref[0, 0:BLK, pair_cols[p]]], axis=0)
                v_span = jnp.concatenate([vm_ref[:, pair_cols[p]], v_ref[0, 0:BLK, pair_cols[p]]], axis=0)
                pv, _ = _sb_span(qq[p], k_span, v_span, rs[p], mm, [meta_valid, first_valid])
                new_accs.append(accs[p] + pv)
            return new_accs

        emit(row0, lax.cond(done, lambda a, r: a, tail, accs, rs))

    def q_group(m, carry):
        for s in range(Q_GROUP):
            row0 = pl.multiple_of((Q_GROUP * m + s) * BLK, BLK)
            if s == 0:
                left_start = pl.multiple_of(jnp.maximum(row0 - BLK, 0), BLK)
                masks = [jnp.logical_and(col >= 0, m >= 1), tri]
            else:
                left_start, masks = pl.multiple_of(row0 - BLK, BLK), [None, tri]
            accs, rs = sweep(load_queries(row0), key_rows([left_start, row0]), masks, zeros, zeros)
            emit(row0, accs)
            for p in range(n_pairs):
                acc_ref[s, p] = accs[p]
                r_ref[s, p] = rs[p]
            done_ref[s] = all_dead(rs).astype(jnp.int32)

        def finish(s, c):
            @pl.when(done_ref[s] == 0)
            def _():
                extend(Q_GROUP * m + s, s)
            return c

        lax.fori_loop(0, Q_GROUP, finish, 0)
        return carry

    lax.fori_loop(0, seq // (Q_GROUP * BLK), q_group, 0)


def _attention(q, k, v, km, vm, sa, mm, n_pairs):
    bsz, seq, d = q.shape
    w = n_pairs * PAIR_W
    seq_blk = pl.BlockSpec((1, seq, w), lambda b, g: (b, 0, g))
    meta_blk = pl.BlockSpec((BLK, w), lambda b, g: (0, g))
    return pl.pallas_call(
        functools.partial(_attention_kernel, n_pairs=n_pairs),
        grid=(bsz, d // w),
        in_specs=[seq_blk, seq_blk, seq_blk, meta_blk, meta_blk, seq_blk,
                  pl.BlockSpec((2 * BLK, 2 * BLK), lambda b, g: (0, 0))],
        out_specs=seq_blk,
        out_shape=jax.ShapeDtypeStruct((bsz, seq, d), BF16),
        scratch_shapes=[pltpu.SMEM((Q_GROUP,), jnp.int32),
                        pltpu.VMEM((Q_GROUP, n_pairs, 2 * BLK, LANES), F32),
                        pltpu.VMEM((Q_GROUP, n_pairs, 2 * BLK, LANES), F32)],
        compiler_params=pltpu.CompilerParams(
            dimension_semantics=("arbitrary", "arbitrary"), vmem_limit_bytes=VMEM_LIMIT),
        name="sb_attention",
    )(q, k, v, km, vm, sa, mm)


def _tail_kernel(og_ref, u_ref, uprev_ref, umeta_ref, sc_ref, ga_ref, gc_ref, res_ref,
                 dww_ref, dwb_ref, cg_ref, cb_ref, wa_ref, wc_ref, wo_ref,
                 pg_ref, pb_ref, out_ref, ubuf, cbuf, *, tm, rows):
    i = pl.program_id(1)
    d = u_ref.shape[2]

    @pl.when(i == 0)
    def _():
        ubuf[0:HALO, :] = umeta_ref[...]

    @pl.when(i > 0)
    def _():
        ubuf[0:HALO, :] = uprev_ref[0]

    ubuf[HALO:HALO + tm, :] = u_ref[0]

    base = HALO - (CONV_K - 1)

    def lane_tile(ci, carry):
        cols = pl.ds(pl.multiple_of(ci * LANES, LANES), LANES)
        bias = dwb_ref[:, cols]
        span = rows + HALO
        for r0 in range(0, tm, rows):
            block = ubuf[r0:r0 + span, cols]
            acc = jnp.zeros((rows, LANES), F32)
            for phase in range(SUBLANES):
                shifted = block if phase == 0 else pltpu.roll(block, span - phase, axis=0)
                for s in range(base, base + CONV_K):
                    if s % SUBLANES == phase:
                        off = s - phase
                        acc = acc + dww_ref[s - base:s - base + 1, cols] * shifted[off:off + rows]
            cbuf[r0:r0 + rows, cols] = acc + bias
        return carry

    lax.fori_loop(0, d // LANES, lane_tile, 0)

    cn = _layer_norm(cbuf[...], cg_ref[...], cb_ref[...])
    cg = (cn * _sigmoid(cn) * sc_ref[0]).astype(BF16)
    y_c = jnp.dot(cg, wc_ref[...], preferred_element_type=F32)
    y_a = jnp.dot(og_ref[0], wa_ref[...], preferred_element_type=F32)
    h = (ga_ref[0] * y_a + gc_ref[0] * y_c).astype(BF16)
    out = jnp.dot(h, wo_ref[...], preferred_element_type=F32)
    out_ref[0] = _layer_norm(res_ref[0] + out, pg_ref[...], pb_ref[...])


def _tail(og, u, umeta, sc, ga, gc, res, dww, dwb, cg, cb, wa, wc, wo, pg, pb, tm):
    bsz, seq, d = res.shape
    tile = pl.BlockSpec((1, tm, d), lambda b, i: (b, i, 0))
    prev = pl.BlockSpec((1, HALO, d), lambda b, i: (b, jnp.maximum(i * (tm // HALO) - 1, 0), 0))
    const = lambda shape: pl.BlockSpec(shape, lambda b, i: (0, 0))
    vec = const((1, d))
    return pl.pallas_call(
        functools.partial(_tail_kernel, tm=tm, rows=64),
        grid=(bsz, seq // tm),
        in_specs=[tile, tile, prev, const((HALO, d)), tile, tile, tile, tile,
                  const((CONV_K, d)), vec, vec, vec,
                  const((d, d)), const((d, d)), const((d, d)), vec, vec],
        out_specs=tile,
        out_shape=jax.ShapeDtypeStruct((bsz, seq, d), F32),
        scratch_shapes=[pltpu.VMEM((HALO + tm, d), F32), pltpu.VMEM((tm, d), F32)],
        compiler_params=pltpu.CompilerParams(
            dimension_semantics=("arbitrary", "arbitrary"), vmem_limit_bytes=VMEM_LIMIT),
        name="conv_merge_out",
    )(og, u, u, umeta, sc, ga, gc, res, dww, dwb, cg, cb, wa, wc, wo, pg, pb)


def _suffix_sum_matrix():
    j = lax.broadcasted_iota(jnp.int32, (BLK, BLK), 0)
    s = lax.broadcasted_iota(jnp.int32, (BLK, BLK), 1)
    half = jnp.concatenate([(j >= s).astype(BF16), jnp.ones((BLK, BLK), BF16)], axis=1)
    return jnp.concatenate([half, half], axis=0)


def kernel(x, meta_tokens, emb_ln_g, emb_ln_b, w_in, dw_w, dw_b, conv_ln_g, conv_ln_b,
           w_attn_out, w_conv_out, w_out, post_ln_g, post_ln_b):
    bsz, seq, d = x.shape
    depth = w_in.shape[0]
    assert depth == 1 and d == N_HEADS * HEAD_DIM and w_in.shape[2] == N_SPLITS * d
    assert meta_tokens.shape == (N_META, d) and dw_w.shape[1] == CONV_K
    assert seq % (Q_GROUP * BLK) == 0 and seq % 512 == 0
    alpha = (2.0 * depth) ** 0.25
    vec = lambda a: a.reshape(1, d).astype(F32)
    eg, eb = vec(emb_ln_g), vec(emb_ln_b)
    w_in_bf = w_in[0].astype(BF16)

    q, k, v, sa, u, sc, ga, gc, res = _ln_proj(x.reshape(bsz * seq, d), eg, eb, w_in_bf, tm=512, alpha=alpha)
    _, km, vm, _, um, _, _, _, _ = _ln_proj(meta_tokens, eg, eb, w_in_bf, tm=N_META, alpha=alpha)
    shape3 = lambda a: a.reshape(bsz, seq, d)

    pad_meta = lambda a, n: jnp.pad(a, ((n - N_META, 0), (0, 0)))
    og = _attention(shape3(q), shape3(k), shape3(v), pad_meta(km, BLK), pad_meta(vm, BLK),
                    shape3(sa), _suffix_sum_matrix(), n_pairs=2)

    return _tail(og, shape3(u), pad_meta(um, HALO), shape3(sc), shape3(ga), shape3(gc), shape3(res),
                 dw_w[0], vec(dw_b[0]), vec(conv_ln_g[0]), vec(conv_ln_b[0]),
                 w_attn_out[0].astype(BF16), w_conv_out[0].astype(BF16), w_out[0].astype(BF16),
                 vec(post_ln_g[0]), vec(post_ln_b[0]), tm=512)
```

```python
import functools

import jax
import jax.numpy as jnp
from jax import lax
from jax.experimental import pallas as pl
from jax.experimental.pallas import tpu as pltpu

F32 = jnp.float32
BF16 = jnp.bfloat16

N_META = 16
N_HEADS = 16
HEAD_DIM = 64
CONV_K = 31
LN_EPS = 1e-5
N_SPLITS = 9

LANES = 128
SUBLANES = 8
BLK = 128
PAIR_W = 2 * HEAD_DIM
HALO = 32
LOG2E = 1.4426950408889634
Q_SCALE = -LOG2E * HEAD_DIM ** -0.5
EXP2_ZERO_BELOW = -127.0
Q_GROUP = 16
VMEM_LIMIT = 56 * 1024 * 1024


def _layer_norm(x, g, b):
    mu = jnp.mean(x, axis=-1, keepdims=True)
    xc = x - mu
    var = jnp.mean(xc * xc, axis=-1, keepdims=True)
    return xc * lax.rsqrt(var + LN_EPS) * g + b


def _sigmoid(x):
    return 1.0 / (1.0 + jnp.exp(-x))


def _ln_proj_kernel(x_ref, g_ref, b_ref, w_ref, q_ref, k_ref, v_ref, sa_ref, u_ref,
                    sc_ref, ga_ref, gc_ref, res_ref, *, d, nc, alpha):
    hn = _layer_norm(x_ref[...], g_ref[...], b_ref[...])
    res_ref[...] = alpha * hn
    hn = hn.astype(BF16)

    def proj(split, c):
        lo = split * d + c * nc
        return jnp.dot(hn, w_ref[:, lo:lo + nc], preferred_element_type=F32)

    for c in range(d // nc):
        cols = slice(c * nc, (c + 1) * nc)
        q_ref[:, cols] = (proj(0, c) * Q_SCALE).astype(BF16)
        k_ref[:, cols] = proj(1, c).astype(BF16)
        v_ref[:, cols] = proj(2, c).astype(BF16)
        z_a = proj(3, c)
        sa_ref[:, cols] = z_a * _sigmoid(z_a)
        u_ref[:, cols] = proj(4, c) * _sigmoid(proj(5, c))
        z_c = proj(6, c)
        sc_ref[:, cols] = z_c * _sigmoid(z_c)
        ga_ref[:, cols] = _sigmoid(proj(7, c))
        gc_ref[:, cols] = _sigmoid(proj(8, c))


def _ln_proj(x2d, g, b, w_bf16, tm, alpha):
    m, d = x2d.shape
    row = lambda i: (i, 0)
    const = lambda i: (0, 0)
    out_sds = lambda dt: jax.ShapeDtypeStruct((m, d), dt)
    return pl.pallas_call(
        functools.partial(_ln_proj_kernel, d=d, nc=min(d, 512), alpha=alpha),
        grid=(m // tm,),
        in_specs=[
            pl.BlockSpec((tm, d), row),
            pl.BlockSpec((1, d), const),
            pl.BlockSpec((1, d), const),
            pl.BlockSpec((d, N_SPLITS * d), const, pipeline_mode=pl.Buffered(1)),
        ],
        out_specs=[pl.BlockSpec((tm, d), row)] * 9,
        out_shape=[out_sds(BF16)] * 3 + [out_sds(F32)] * 6,
        compiler_params=pltpu.CompilerParams(
            dimension_semantics=("arbitrary",), vmem_limit_bytes=VMEM_LIMIT),
        name="ln_proj",
    )(x2d, g, b, w_bf16)


def _sb_span(qq, k_span, v_span, r, mm, masks):
    nb = len(masks)
    nz = lax.dot_general(qq, k_span, (((1,), (1,)), ((), ())), preferred_element_type=F32)
    nzs, parts = [], []
    for j in range(nb):
        nzj = nz[:, j * BLK:(j + 1) * BLK]
        neg_abs = lax.bitcast_convert_type(
            lax.bitcast_convert_type(nzj, jnp.uint32) | jnp.uint32(0x80000000), F32)
        sp = jnp.log(1.0 + jnp.exp2(neg_abs))
        l1m = jnp.minimum(nzj, 0.0) - sp * LOG2E
        if masks[j] is not None:
            l1m = jnp.where(masks[j], l1m, 0.0)
        hi = l1m.astype(BF16)
        lo = (l1m - hi.astype(F32)).astype(BF16)
        nzs.append(nzj)
        parts.append(jnp.concatenate([hi, lo], axis=1))
    st = jnp.dot(jnp.concatenate(parts, axis=0), mm, preferred_element_type=F32)
    attn = [None] * nb
    for j in reversed(range(nb)):
        st_j = st[j * 2 * BLK:(j + 1) * 2 * BLK]
        a = jnp.exp2(st_j[:, :BLK] - nzs[j] + r)
        if masks[j] is not None:
            a = jnp.where(masks[j], a, 0.0)
        attn[j] = a.astype(BF16)
        r = r + st_j[:, BLK:]
    pv = jnp.dot(jnp.concatenate(attn, axis=1), v_span, preferred_element_type=F32)
    return pv, r


def _attention_kernel(q_ref, k_ref, v_ref, km_ref, vm_ref, sa_ref, mm_ref, o_ref, done_ref, acc_ref, r_ref,
                      *, n_pairs):
    seq = q_ref.shape[1]
    mm = mm_ref[...]
    lane = lax.broadcasted_iota(jnp.int32, (BLK, LANES), 1)
    row = lax.broadcasted_iota(jnp.int32, (2 * BLK, BLK), 0) % BLK
    col = lax.broadcasted_iota(jnp.int32, (2 * BLK, BLK), 1)
    first_head = lane < HEAD_DIM
    tri = col < row
    meta_valid = col >= BLK - N_META
    pair_cols = [slice(p * PAIR_W, (p + 1) * PAIR_W) for p in range(n_pairs)]
    zeros = [jnp.zeros((2 * BLK, LANES), F32)] * n_pairs

    def load_queries(row0):
        qq = []
        for p in range(n_pairs):
            q2 = q_ref[0, pl.ds(row0, BLK), pair_cols[p]]
            zero = jnp.zeros_like(q2)
            qq.append(jnp.concatenate([jnp.where(first_head, q2, zero),
                                       jnp.where(first_head, zero, q2)], axis=0))
        return qq

    def key_rows(starts):
        out = []
        for p in range(n_pairs):
            ks = [k_ref[0, pl.ds(st, BLK), pair_cols[p]] for st in starts]
            vs = [v_ref[0, pl.ds(st, BLK), pair_cols[p]] for st in starts]
            out.append((jnp.concatenate(ks, axis=0), jnp.concatenate(vs, axis=0)))
        return out

    def sweep(qq, kvs, masks, accs, rs):
        out = [_sb_span(qq[p], kvs[p][0], kvs[p][1], rs[p], mm, masks) for p in range(n_pairs)]
        return [a + o[0] for a, o in zip(accs, out)], [o[1] for o in out]

    def all_dead(rs):
        worst = rs[0]
        for r in rs[1:]:
            worst = jnp.maximum(worst, r)
        return jnp.max(worst) < EXP2_ZERO_BELOW

    def emit(row0, accs):
        for p in range(n_pairs):
            o = jnp.where(first_head, accs[p][:BLK], accs[p][BLK:])
            gate = sa_ref[0, pl.ds(row0, BLK), pair_cols[p]]
            o_ref[0, pl.ds(row0, BLK), pair_cols[p]] = (o * gate).astype(BF16)

    def extend(qb, s):
        row0 = pl.multiple_of(qb * BLK, BLK)
        qq = load_queries(row0)
        accs = [acc_ref[s, p] for p in range(n_pairs)]
        rs = [r_ref[s, p] for p in range(n_pairs)]
        left = jnp.maximum(qb - 1, 0)

        def cond(state):
            i, done, _, _ = state
            return jnp.logical_and(i < left // 2, jnp.logical_not(done))

        def body(state):
            i, _, accs, rs = state
            start = pl.multiple_of(row0 - (2 * i + 3) * BLK, BLK)
            accs, rs = sweep(qq, key_rows([start, start + BLK]), [None, None], accs, rs)
            return i + 1, all_dead(rs), accs, rs

        _, done, accs, rs = lax.while_loop(cond, body, (0, False, accs, rs))

        def tail(accs, rs):
            first_valid = jnp.logical_and(col >= 0, (left % 2) == 1)
            new_accs = []
            for p in range(n_pairs):
                k_span = jnp.concatenate([km_ref[:, pair_cols[p]], k_ref[0, 0:BLK, pair_cols[p]]], axis=0)
                v_span = jnp.concatenate([vm_ref[:, pair_cols[p]], v_ref[0, 0:BLK, pair_cols[p]]], axis=0)
                pv, _ = _sb_span(qq[p], k_span, v_span, rs[p], mm, [meta_valid, first_valid])
                new_accs.append(accs[p] + pv)
            return new_accs

        emit(row0, lax.cond(done, lambda a, r: a, tail, accs, rs))

    def q_group(m, carry):
        for s in range(Q_GROUP):
            row0 = pl.multiple_of((Q_GROUP * m + s) * BLK, BLK)
            if s == 0:
                left_start = pl.multiple_of(jnp.maximum(row0 - BLK, 0), BLK)
                masks = [jnp.logical_and(col >= 0, m >= 1), tri]
            else:
                left_start, masks = pl.multiple_of(row0 - BLK, BLK), [None, tri]
            accs, rs = sweep(load_queries(row0), key_rows([left_start, row0]), masks, zeros, zeros)
            emit(row0, accs)
            for p in range(n_pairs):
                acc_ref[s, p] = accs[p]
                r_ref[s, p] = rs[p]
            done_ref[s] = all_dead(rs).astype(jnp.int32)

        def finish(s, c):
            @pl.when(done_ref[s] == 0)
            def _():
                extend(Q_GROUP * m + s, s)
            return c

        lax.fori_loop(0, Q_GROUP, finish, 0)
        return carry

    lax.fori_loop(0, seq // (Q_GROUP * BLK), q_group, 0)


def _attention(q, k, v, km, vm, sa, mm, n_pairs):
    bsz, seq, d = q.shape
    w = n_pairs * PAIR_W
    seq_blk = pl.BlockSpec((1, seq, w), lambda b, g: (b, 0, g))
    meta_blk = pl.BlockSpec((BLK, w), lambda b, g: (0, g))
    return pl.pallas_call(
        functools.partial(_attention_kernel, n_pairs=n_pairs),
        grid=(bsz, d // w),
        in_specs=[seq_blk, seq_blk, seq_blk, meta_blk, meta_blk, seq_blk,
                  pl.BlockSpec((2 * BLK, 2 * BLK), lambda b, g: (0, 0))],
        out_specs=seq_blk,
        out_shape=jax.ShapeDtypeStruct((bsz, seq, d), BF16),
        scratch_shapes=[pltpu.SMEM((Q_GROUP,), jnp.int32),
                        pltpu.VMEM((Q_GROUP, n_pairs, 2 * BLK, LANES), F32),
                        pltpu.VMEM((Q_GROUP, n_pairs, 2 * BLK, LANES), F32)],
        compiler_params=pltpu.CompilerParams(
            dimension_semantics=("arbitrary", "arbitrary"), vmem_limit_bytes=VMEM_LIMIT),
        name="sb_attention",
    )(q, k, v, km, vm, sa, mm)


def _tail_kernel(og_ref, u_ref, uprev_ref, umeta_ref, sc_ref, ga_ref, gc_ref, res_ref,
                 dww_ref, dwb_ref, cg_ref, cb_ref, wa_ref, wc_ref, wo_ref,
                 pg_ref, pb_ref, out_ref, ubuf, cbuf, *, tm, rows):
    i = pl.program_id(1)
    d = u_ref.shape[2]

    @pl.when(i == 0)
    def _():
        ubuf[0:HALO, :] = umeta_ref[...]

    @pl.when(i > 0)
    def _():
        ubuf[0:HALO, :] = uprev_ref[0]

    ubuf[HALO:HALO + tm, :] = u_ref[0]

    base = HALO - (CONV_K - 1)

    def lane_tile(ci, carry):
        cols = pl.ds(pl.multiple_of(ci * LANES, LANES), LANES)
        bias = dwb_ref[:, cols]
        span = rows + HALO
        for r0 in range(0, tm, rows):
            block = ubuf[r0:r0 + span, cols]
            acc = jnp.zeros((rows, LANES), F32)
            for phase in range(SUBLANES):
                shifted = block if phase == 0 else pltpu.roll(block, span - phase, axis=0)
                for s in range(base, base + CONV_K):
                    if s % SUBLANES == phase:
                        off = s - phase
                        acc = acc + dww_ref[s - base:s - base + 1, cols] * shifted[off:off + rows]
            cbuf[r0:r0 + rows, cols] = acc + bias
        return carry

    lax.fori_loop(0, d // LANES, lane_tile, 0)

    cn = _layer_norm(cbuf[...], cg_ref[...], cb_ref[...])
    cg = (cn * _sigmoid(cn) * sc_ref[0]).astype(BF16)
    y_c = jnp.dot(cg, wc_ref[...], preferred_element_type=F32)
    y_a = jnp.dot(og_ref[0], wa_ref[...], preferred_element_type=F32)
    h = (ga_ref[0] * y_a + gc_ref[0] * y_c).astype(BF16)
    out = jnp.dot(h, wo_ref[...], preferred_element_type=F32)
    out_ref[0] = _layer_norm(res_ref[0] + out, pg_ref[...], pb_ref[...])


def _tail(og, u, umeta, sc, ga, gc, res, dww, dwb, cg, cb, wa, wc, wo, pg, pb, tm):
    bsz, seq, d = res.shape
    tile = pl.BlockSpec((1, tm, d), lambda b, i: (b, i, 0))
    prev = pl.BlockSpec((1, HALO, d), lambda b, i: (b, jnp.maximum(i * (tm // HALO) - 1, 0), 0))
    const = lambda shape: pl.BlockSpec(shape, lambda b, i: (0, 0))
    vec = const((1, d))
    return pl.pallas_call(
        functools.partial(_tail_kernel, tm=tm, rows=64),
        grid=(bsz, seq // tm),
        in_specs=[tile, tile, prev, const((HALO, d)), tile, tile, tile, tile,
                  const((CONV_K, d)), vec, vec, vec,
                  const((d, d)), const((d, d)), const((d, d)), vec, vec],
        out_specs=tile,
        out_shape=jax.ShapeDtypeStruct((bsz, seq, d), F32),
        scratch_shapes=[pltpu.VMEM((HALO + tm, d), F32), pltpu.VMEM((tm, d), F32)],
        compiler_params=pltpu.CompilerParams(
            dimension_semantics=("arbitrary", "arbitrary"), vmem_limit_bytes=VMEM_LIMIT),
        name="conv_merge_out",
    )(og, u, u, umeta, sc, ga, gc, res, dww, dwb, cg, cb, wa, wc, wo, pg, pb)


def _suffix_sum_matrix():
    j = lax.broadcasted_iota(jnp.int32, (BLK, BLK), 0)
    s = lax.broadcasted_iota(jnp.int32, (BLK, BLK), 1)
    half = jnp.concatenate([(j >= s).astype(BF16), jnp.ones((BLK, BLK), BF16)], axis=1)
    return jnp.concatenate([half, half], axis=0)


def kernel(x, meta_tokens, emb_ln_g, emb_ln_b, w_in, dw_w, dw_b, conv_ln_g, conv_ln_b,
           w_attn_out, w_conv_out, w_out, post_ln_g, post_ln_b):
    bsz, seq, d = x.shape
    depth = w_in.shape[0]
    assert depth == 1 and d == N_HEADS * HEAD_DIM and w_in.shape[2] == N_SPLITS * d
    assert meta_tokens.shape == (N_META, d) and dw_w.shape[1] == CONV_K
    assert seq % (Q_GROUP * BLK) == 0 and seq % 512 == 0
    alpha = (2.0 * depth) ** 0.25
    vec = lambda a: a.reshape(1, d).astype(F32)
    eg, eb = vec(emb_ln_g), vec(emb_ln_b)
    w_in_bf = w_in[0].astype(BF16)

    q, k, v, sa, u, sc, ga, gc, res = _ln_proj(x.reshape(bsz * seq, d), eg, eb, w_in_bf, tm=512, alpha=alpha)
    _, km, vm, _, um, _, _, _, _ = _ln_proj(meta_tokens, eg, eb, w_in_bf, tm=N_META, alpha=alpha)
    shape3 = lambda a: a.reshape(bsz, seq, d)

    pad_meta = lambda a, n: jnp.pad(a, ((n - N_META, 0), (0, 0)))
    og = _attention(shape3(q), shape3(k), shape3(v), pad_meta(km, BLK), pad_meta(vm, BLK),
                    shape3(sa), _suffix_sum_matrix(), n_pairs=2)

    return _tail(og, shape3(u), pad_meta(um, HALO), shape3(sc), shape3(ga), shape3(gc), shape3(res),
                 dw_w[0], vec(dw_b[0]), vec(conv_ln_g[0]), vec(conv_ln_b[0]),
                 w_attn_out[0].astype(BF16), w_conv_out[0].astype(BF16), w_out[0].astype(BF16),
                 vec(post_ln_g[0]), vec(post_ln_b[0]), tm=512)
```
